```python
import functools
import jax
import jax.numpy as jnp
from jax import lax
import numpy as np

D_MODEL = 1024
BATCH = 2
SEQ = 8192
DEPTH = 2
DEC_BATCH = 32
DEC_SEQ = 4
PAST_LEN = 16384
PAGE_SIZE = 128

N_EVEN = (DEPTH + 1) // 2
N_ODD = DEPTH // 2

SB_HEADS = 8
SB_DIM = 64
SB_W = SB_HEADS * SB_DIM
SB_QBLOCK = 128
ML_HEADS = 4
ML_DIM = 128
ML_W = ML_HEADS * ML_DIM
ML_CHUNK = 64
GLA_HEADS = 4
GLA_DK = 64
GLA_DV = 128
GLA_KW = GLA_HEADS * GLA_DK
GLA_VW = GLA_HEADS * GLA_DV
GLA_RANK = 16
GLA_TAU = 16.0
GLA_CHUNK = 64
MB_HEADS = 8
MB_DIM = 64
MB_W = MB_HEADS * MB_DIM
MB_BLOCK = 256
MB_TOPK = 3
MB_QBLOCK = 64

EVEN_SPLITS = (SB_W, SB_W, SB_W, SB_W, ML_W, ML_W, ML_W, ML_W, ML_W, ML_HEADS, ML_HEADS)
ODD_SPLITS = (GLA_KW, GLA_KW, GLA_VW, GLA_VW, GLA_RANK, MB_W, MB_W, MB_W, MB_W)
EVEN_IN = sum(EVEN_SPLITS)
ODD_IN = sum(ODD_SPLITS)
EVEN_MIX = SB_W + ML_W
ODD_MIX = GLA_VW + MB_W
EPS = 1e-6

kernel_name = "hybrid_sb_mlstm_gla_moba_step"


def rms_norm(x, g):
    xf = x.astype(jnp.float32)
    y = xf * lax.rsqrt(jnp.mean(xf * xf, axis=-1, keepdims=True) + EPS)
    return (y * g.astype(jnp.float32)).astype(x.dtype)


def split_cols(u, sizes):
    return jnp.split(u, np.cumsum(sizes)[:-1].tolist(), axis=-1)


def alibi_slopes(n_heads):
    return jnp.exp2(-(8.0 / n_heads) * jnp.arange(1, n_heads + 1, dtype=jnp.float32))


def ada_ln(x, c, g, w_mod, b_mod):
    shift, scale, gate = jnp.split(jax.nn.silu(c) @ w_mod + b_mod, 3, axis=-1)
    h = rms_norm(x, g) * (1.0 + scale[:, None]) + shift[:, None]
    return h, gate[:, None]


def chunked_scan(step, state0, xs, chunk):
    b, s = xs[0].shape[:2]
    nc = s // chunk
    xs_c = tuple(a.reshape((b, nc, chunk) + a.shape[2:]).swapaxes(0, 1) for a in xs)
    state, ys = lax.scan(step, state0, xs_c)
    return ys.swapaxes(0, 1).reshape((b, s) + ys.shape[3:]), state


def stick_breaking(q, q_pos, k, v, k_pos):
    z = jnp.einsum('bqhd,bkhd->bhqk', q, k).astype(jnp.float32) * (SB_DIM ** -0.5)
    mask = k_pos[None, :] < q_pos[:, None]
    log1m = jnp.where(mask, jax.nn.log_sigmoid(-z), 0.0)
    after = lax.cumsum(log1m, axis=3, reverse=True) - log1m
    w = jnp.where(mask, jnp.exp(jax.nn.log_sigmoid(z) + after), 0.0)
    return jnp.einsum('bhqk,bkhd->bqhd', w.astype(v.dtype), v)


def sb_prompt(q, k, v):
    b, s = q.shape[:2]
    pos = jnp.arange(s)
    nblk = s // SB_QBLOCK
    qb = q.reshape(b, nblk, SB_QBLOCK, SB_HEADS, SB_DIM).swapaxes(0, 1)

    def one(args):
        q_blk, s0 = args
        return stick_breaking(q_blk, s0 + jnp.arange(SB_QBLOCK), k, v, pos)

    out = lax.map(one, (qb, jnp.arange(nblk) * SB_QBLOCK))
    return out.swapaxes(0, 1).reshape(b, s, SB_HEADS, SB_DIM)


def sb_with_past(q, k, v, k_past, v_past):
    past, t = k_past.shape[1], q.shape[1]
    kk = jnp.concatenate([k_past.astype(k.dtype), k], axis=1)
    vv = jnp.concatenate([v_past.astype(v.dtype), v], axis=1)
    return stick_breaking(q, past + jnp.arange(t), kk, vv, jnp.arange(past + t))


def mlstm_chunk(state, inp):
    c0, n0, m0 = state
    q, k, v, ig, lf = inp
    L = q.shape[1]
    bc = jnp.cumsum(lf, axis=1)
    a = ig - bc
    m = jnp.maximum(bc + m0[:, None], lax.cummax(a, axis=1) + bc)
    causal = jnp.tril(jnp.ones((L, L), dtype=bool))
    log_d = bc[:, :, None, :] + a[:, None, :, :] - m[:, :, None, :]
    dmat = jnp.exp(jnp.where(causal[None, :, :, None], log_d, -jnp.inf))
    wts = dmat * jnp.einsum('bthd,bshd->btsh', q, k)
    inter = jnp.exp(bc + m0[:, None] - m)
    num = inter[..., None] * jnp.einsum('bthd,bhde->bthe', q, c0) + jnp.einsum('btsh,bshe->bthe', wts, v)
    den = inter * jnp.einsum('bthd,bhd->bth', q, n0) + wts.sum(axis=2)
    h = num / jnp.maximum(jnp.abs(den), jnp.exp(-m))[..., None]
    m_l = m[:, -1]
    decay = jnp.exp(bc[:, -1] + m0 - m_l)
    ws = jnp.exp(bc[:, -1:] + a - m_l[:, None])
    c1 = decay[..., None, None] * c0 + jnp.einsum('bsh,bshd,bshe->bhde', ws, k, v)
    n1 = decay[..., None] * n0 + jnp.einsum('bsh,bshd->bhd', ws, k)
    return (c1, n1, m_l), h


def gla_chunk(s0, inp):
    q, k, v, lg = inp
    L = q.shape[1]
    g = jnp.cumsum(lg, axis=1)
    causal = jnp.tril(jnp.ones((L, L), dtype=bool))
    dec = jnp.exp(jnp.where(causal[None, :, :, None, None], g[:, :, None] - g[:, None, :], -jnp.inf))
    att = jnp.einsum('bthd,bshd,btshd->btsh', q, k, dec)
    o = jnp.einsum('bthd,bhde->bthe', q * jnp.exp(g), s0) + jnp.einsum('btsh,bshe->bthe', att, v)
    s1 = jnp.exp(g[:, -1])[..., None] * s0 + jnp.einsum('bshd,bshe->bhde', k * jnp.exp(g[:, -1:] - g), v)
    return s1, o


def moba_select(q, kmean, cur, nb_valid):
    s = jnp.einsum('bthd,bnhd->bthn', q, kmean).astype(jnp.float32)
    blk = jnp.arange(kmean.shape[1])
    ok = (blk[None, :] < cur[:, None]) & (blk[None, :] < nb_valid)
    s = jnp.where(ok[None, :, None, :], s, -jnp.inf)
    _, idx = lax.top_k(s, MB_TOPK)
    sel_ok = (idx < cur[None, :, None, None]) & (idx < nb_valid)
    return idx, sel_ok


def moba_attend(q, q_pos, sel_k, sel_v, sel_pos, sel_ok, own_k, own_v, own_pos, slopes):
    b, tq = q.shape[:2]
    scale = MB_DIM ** -0.5
    dist_sel = (q_pos[None, :, None, None, None] - sel_pos).astype(jnp.float32)
    s_sel = jnp.einsum('bthd,bthknd->bthkn', q, sel_k).astype(jnp.float32) * scale - slopes[:, None, None] * dist_sel
    s_sel = jnp.where(sel_ok[..., None], s_sel, -jnp.inf).reshape(b, tq, MB_HEADS, -1)
    dist_own = (q_pos[:, None] - own_pos[None, :]).astype(jnp.float32)
    s_own = jnp.einsum('bthd,bshd->bths', q, own_k).astype(jnp.float32) * scale - slopes[:, None] * dist_own[:, None, :]
    own_ok = (own_pos[None, :] >= (q_pos // MB_BLOCK * MB_BLOCK)[:, None]) & (own_pos[None, :] <= q_pos[:, None])
    s_own = jnp.where(own_ok[:, None, :], s_own, -jnp.inf)
    p = jax.nn.softmax(jnp.concatenate([s_sel, s_own], axis=-1), axis=-1)
    n_sel = s_sel.shape[-1]
    p_sel = p[..., :n_sel].reshape(b, tq, MB_HEADS, MB_TOPK, -1).astype(sel_v.dtype)
    p_own = p[..., n_sel:].astype(own_v.dtype)
    out = jnp.einsum('bthkn,bthknd->bthd', p_sel, sel_v) + jnp.einsum('bths,bshd->bthd', p_own, own_v)
    return out.astype(q.dtype)


def moba_prompt(q, k, v, slopes):
    b, s = q.shape[:2]
    nb_full = s // MB_BLOCK
    nb_pad = max(-(-s // MB_BLOCK), MB_TOPK)
    pad = ((0, 0), (0, nb_pad * MB_BLOCK - s), (0, 0), (0, 0))
    kp, vp = jnp.pad(k, pad), jnp.pad(v, pad)
    kb = kp.reshape(b, nb_pad, MB_BLOCK, MB_HEADS, MB_DIM)
    vb = vp.reshape(b, nb_pad, MB_BLOCK, MB_HEADS, MB_DIM)
    kmean = jnp.mean(kb, axis=2, dtype=jnp.float32)
    kbt = kb.transpose(0, 1, 3, 2, 4)
    vbt = vb.transpose(0, 1, 3, 2, 4)
    bidx = jnp.arange(b)[:, None, None, None]
    hidx = jnp.arange(MB_HEADS)[None, None, :, None]
    blk = jnp.arange(MB_BLOCK)
    nq = s // MB_QBLOCK
    qb = q.reshape(b, nq, MB_QBLOCK, MB_HEADS, MB_DIM).swapaxes(0, 1)

    def one(args):
        q_blk, s0 = args
        q_pos = s0 + jnp.arange(MB_QBLOCK)
        idx, ok = moba_select(q_blk, kmean, q_pos // MB_BLOCK, nb_full)
        sel_k = kbt[bidx, idx, hidx]
        sel_v = vbt[bidx, idx, hidx]
        sel_pos = idx[..., None] * MB_BLOCK + blk
        own0 = (s0 // MB_BLOCK) * MB_BLOCK
        own_k = lax.dynamic_slice_in_dim(kp, own0, MB_BLOCK, axis=1)
        own_v = lax.dynamic_slice_in_dim(vp, own0, MB_BLOCK, axis=1)
        return moba_attend(q_blk, q_pos, sel_k, sel_v, sel_pos, ok, own_k, own_v, own0 + blk, slopes)

    out = lax.map(one, (qb, jnp.arange(nq) * MB_QBLOCK))
    return out.swapaxes(0, 1).reshape(b, s, MB_HEADS, MB_DIM)


def moba_sample(q, k, v, cache_k, cache_v, page_means, layer, page_table, slopes):
    db, t = q.shape[:2]
    n_pages = page_table.shape[1]
    past = n_pages * PAGE_SIZE
    ppb = MB_BLOCK // PAGE_SIZE
    nb_full = n_pages // ppb
    nb_pad = max(nb_full, MB_TOPK)
    pmean = page_means[layer, page_table]
    kmean = pmean[:, :nb_full * ppb].reshape(db, nb_full, ppb, MB_HEADS, MB_DIM).mean(axis=2)
    kmean = jnp.pad(kmean, ((0, 0), (0, nb_pad - nb_full), (0, 0), (0, 0)))
    q_pos = past + jnp.arange(t)
    idx, ok = moba_select(q, kmean, q_pos // MB_BLOCK, nb_full)
    lp = jnp.minimum(idx[..., None] * ppb + jnp.arange(ppb), n_pages - 1)
    phys = page_table[jnp.arange(db)[:, None, None, None, None], lp]
    hidx = jnp.arange(MB_HEADS)[None, None, :, None, None]
    sel_k = cache_k[layer, phys, :, hidx].reshape(db, t, MB_HEADS, MB_TOPK, MB_BLOCK, MB_DIM)
    sel_v = cache_v[layer, phys, :, hidx].reshape(db, t, MB_HEADS, MB_TOPK, MB_BLOCK, MB_DIM)
    sel_pos = idx[..., None] * MB_BLOCK + jnp.arange(MB_BLOCK)
    last = page_table[:, n_pages - 1]
    own_k = jnp.concatenate([cache_k[layer, last].astype(k.dtype), k], axis=1)
    own_v = jnp.concatenate([cache_v[layer, last].astype(v.dtype), v], axis=1)
    own_pos = past - PAGE_SIZE + jnp.arange(PAGE_SIZE + t)
    return moba_attend(q, q_pos, sel_k, sel_v, sel_pos, ok, own_k, own_v, own_pos, slopes)


def even_layer(h, w_in, w_out, b_i, b_f, gn, sb_fn, ml_state):
    b, t, _ = h.shape
    f32 = jnp.float32
    sq, sk, sv, sz, mq, mk, mv, mo, mz, mi, mf = split_cols(h @ w_in, EVEN_SPLITS)
    k_a = sk.reshape(b, t, SB_HEADS, SB_DIM)
    v_a = sv.reshape(b, t, SB_HEADS, SB_DIM)
    a_out = sb_fn(sq.reshape(b, t, SB_HEADS, SB_DIM), k_a, v_a).reshape(b, t, SB_W) * jax.nn.silu(sz)
    qb = mq.reshape(b, t, ML_HEADS, ML_DIM).astype(f32)
    kb = mk.reshape(b, t, ML_HEADS, ML_DIM).astype(f32) * (ML_DIM ** -0.5)
    vb = mv.reshape(b, t, ML_HEADS, ML_DIM).astype(f32)
    ig = (mi + b_i).astype(f32)
    lf = jax.nn.log_sigmoid((mf + b_f).astype(f32))
    chunk = ML_CHUNK if t % ML_CHUNK == 0 else t
    hb, new_state = chunked_scan(mlstm_chunk, ml_state, (qb, kb, vb, ig, lf), chunk)
    og = jax.nn.sigmoid(mo.astype(f32)).reshape(b, t, ML_HEADS, ML_DIM)
    b_out = (rms_norm(hb, gn) * og).reshape(b, t, ML_W).astype(h.dtype) * jax.nn.silu(mz)
    y = jnp.concatenate([a_out, b_out], axis=-1) @ w_out
    return y, k_a, v_a, new_state


def odd_layer(h, w_in, w_out, w_g2, b_g, gn, gla_state, moba_fn):
    b, t, _ = h.shape
    f32 = jnp.float32
    gq, gk, gv, gz, gg, d_q, d_k, d_v, d_z = split_cols(h @ w_in, ODD_SPLITS)
    qc = gq.reshape(b, t, GLA_HEADS, GLA_DK).astype(f32) * (GLA_DK ** -0.5)
    kc = gk.reshape(b, t, GLA_HEADS, GLA_DK).astype(f32)
    vc = gv.reshape(b, t, GLA_HEADS, GLA_DV).astype(f32)
    lg = (jax.nn.log_sigmoid((gg @ w_g2 + b_g).astype(f32)) / GLA_TAU).reshape(b, t, GLA_HEADS, GLA_DK)
    chunk = GLA_CHUNK if t % GLA_CHUNK == 0 else t
    oc, gla_new = chunked_scan(gla_chunk, gla_state, (qc, kc, vc, lg), chunk)
    c_out = rms_norm(oc, gn).reshape(b, t, GLA_VW).astype(h.dtype) * jax.nn.silu(gz)
    kd = d_k.reshape(b, t, MB_HEADS, MB_DIM)
    vd = d_v.reshape(b, t, MB_HEADS, MB_DIM)
    d_out = moba_fn(d_q.reshape(b, t, MB_HEADS, MB_DIM), kd, vd).reshape(b, t, MB_W) * jax.nn.silu(d_z)
    y = jnp.concatenate([c_out, d_out], axis=-1) @ w_out
    return y, gla_new, kd, vd


def setup_inputs(seed: int = 0) -> dict:
    key = jax.random.key(seed)
    ks = jax.random.split(key, 32)
    f32 = jnp.float32
    n_pages = PAST_LEN // PAGE_SIZE
    n_used = DEC_BATCH * n_pages
    n_pool = n_used + n_used // 4 + 1

    def nrm(k, shape, s=1.0):
        return s * jax.random.normal(k, shape, f32)

    return {
        "x_prompt": nrm(ks[0], (BATCH, SEQ, D_MODEL)),
        "x_sample": nrm(ks[1], (DEC_BATCH, DEC_SEQ, D_MODEL)),
        "cache_sb_k": nrm(ks[2], (N_EVEN, n_pool, PAGE_SIZE, SB_HEADS, SB_DIM)),
        "cache_sb_v": nrm(ks[3], (N_EVEN, n_pool, PAGE_SIZE, SB_HEADS, SB_DIM)),
        "cache_mb_k": nrm(ks[4], (N_ODD, n_pool, PAGE_SIZE, MB_HEADS, MB_DIM)),
        "cache_mb_v": nrm(ks[5], (N_ODD, n_pool, PAGE_SIZE, MB_HEADS, MB_DIM)),
        "state_ml_C": nrm(ks[6], (N_EVEN, DEC_BATCH, ML_HEADS, ML_DIM, ML_DIM), 0.3),
        "state_ml_n": nrm(ks[7], (N_EVEN, DEC_BATCH, ML_HEADS, ML_DIM), 0.3),
        "state_ml_m": nrm(ks[8], (N_EVEN, DEC_BATCH, ML_HEADS)),
        "state_gla_S": nrm(ks[9], (N_ODD, DEC_BATCH, GLA_HEADS, GLA_DK, GLA_DV), 0.3),
        "page_table": jax.random.permutation(ks[10], n_pool)[:n_used].reshape(DEC_BATCH, n_pages).astype(jnp.int32),
        "c_prompt": nrm(ks[11], (BATCH, D_MODEL)),
        "c_sample": nrm(ks[12], (DEC_BATCH, D_MODEL)),
        "w_mod": nrm(ks[13], (DEPTH, D_MODEL, 3 * D_MODEL), 0.5 * D_MODEL ** -0.5),
        "b_mod": nrm(ks[14], (DEPTH, 3 * D_MODEL), 0.02),
        "norm_g": 1.0 + nrm(ks[15], (DEPTH, D_MODEL), 0.01),
        "final_g": 1.0 + nrm(ks[16], (D_MODEL,), 0.01),
        "w_in_even": nrm(ks[17], (N_EVEN, D_MODEL, EVEN_IN), D_MODEL ** -0.5),
        "w_out_even": nrm(ks[18], (N_EVEN, EVEN_MIX, D_MODEL), EVEN_MIX ** -0.5),
        "ml_b_i": -1.0 + nrm(ks[19], (N_EVEN, ML_HEADS), 0.1),
        "ml_b_f": 3.0 + nrm(ks[20], (N_EVEN, ML_HEADS), 0.1),
        "ml_gn": 1.0 + nrm(ks[21], (N_EVEN, ML_HEADS, ML_DIM), 0.01),
        "w_in_odd": nrm(ks[22], (N_ODD, D_MODEL, ODD_IN), D_MODEL ** -0.5),
        "w_out_odd": nrm(ks[23], (N_ODD, ODD_MIX, D_MODEL), ODD_MIX ** -0.5),
        "gla_w_g2": nrm(ks[24], (N_ODD, GLA_RANK, GLA_KW), GLA_RANK ** -0.5),
        "gla_b_g": nrm(ks[25], (N_ODD, GLA_KW), 0.02),
        "gla_gn": 1.0 + nrm(ks[26], (N_ODD, GLA_HEADS, GLA_DV), 0.01),
    }


def reference(x_prompt, x_sample, cache_sb_k, cache_sb_v, cache_mb_k, cache_mb_v,
              state_ml_C, state_ml_n, state_ml_m, state_gla_S, page_table, c_prompt, c_sample,
              w_mod, b_mod, norm_g, final_g, w_in_even, w_out_even, ml_b_i, ml_b_f, ml_gn,
              w_in_odd, w_out_odd, gla_w_g2, gla_b_g, gla_gn):
    f32 = jnp.float32
    bsz = x_prompt.shape[0]
    db = x_sample.shape[0]
    past = page_table.shape[1] * PAGE_SIZE
    slopes = alibi_slopes(MB_HEADS)
    page_means = jnp.mean(cache_mb_k, axis=2, dtype=f32)
    xp, xs = x_prompt, x_sample
    sbk_p, sbv_p, sbk_s, sbv_s = [], [], [], []
    mlc_p, mln_p, mlm_p, mlc_s, mln_s, mlm_s = [], [], [], [], [], []
    glas_p, glas_s = [], []
    mbk_p, mbv_p, mbk_s, mbv_s = [], [], [], []
    for l in range(DEPTH):
        j = l // 2
        hp, gp = ada_ln(xp, c_prompt, norm_g[l], w_mod[l], b_mod[l])
        hs, gs = ada_ln(xs, c_sample, norm_g[l], w_mod[l], b_mod[l])
        if l % 2 == 0:
            ml0 = (jnp.zeros((bsz, ML_HEADS, ML_DIM, ML_DIM), f32), jnp.zeros((bsz, ML_HEADS, ML_DIM), f32),
                   jnp.zeros((bsz, ML_HEADS), f32))
            yp, k_p, v_p, (c_p, n_p, m_p) = even_layer(hp, w_in_even[j], w_out_even[j], ml_b_i[j], ml_b_f[j],
                                                      ml_gn[j], sb_prompt, ml0)
            k_past = cache_sb_k[j, page_table].reshape(db, past, SB_HEADS, SB_DIM)
            v_past = cache_sb_v[j, page_table].reshape(db, past, SB_HEADS, SB_DIM)
            sb_fn = functools.partial(sb_with_past, k_past=k_past, v_past=v_past)
            ml_s = (state_ml_C[j].astype(f32), state_ml_n[j].astype(f32), state_ml_m[j].astype(f32))
            ys, k_s, v_s, (c_s, n_s, m_s) = even_layer(hs, w_in_even[j], w_out_even[j], ml_b_i[j], ml_b_f[j],
                                                      ml_gn[j], sb_fn, ml_s)
            sbk_p.append(k_p); sbv_p.append(v_p); sbk_s.append(k_s); sbv_s.append(v_s)
            mlc_p.append(c_p); mln_p.append(n_p); mlm_p.append(m_p)
            mlc_s.append(c_s); mln_s.append(n_s); mlm_s.append(m_s)
        else:
            s0 = jnp.zeros((bsz, GLA_HEADS, GLA_DK, GLA_DV), f32)
            yp, s_p, k_p, v_p = odd_layer(hp, w_in_odd[j], w_out_odd[j], gla_w_g2[j], gla_b_g[j], gla_gn[j], s0,
                                          functools.partial(moba_prompt, slopes=slopes))
            mfn = functools.partial(moba_sample, cache_k=cache_mb_k, cache_v=cache_mb_v, page_means=page_means,
                                    layer=j, page_table=page_table, slopes=slopes)
            ys, s_s, k_s, v_s = odd_layer(hs, w_in_odd[j], w_out_odd[j], gla_w_g2[j], gla_b_g[j], gla_gn[j],
                                          state_gla_S[j].astype(f32), mfn)
            glas_p.append(s_p); glas_s.append(s_s)
            mbk_p.append(k_p); mbv_p.append(v_p); mbk_s.append(k_s); mbv_s.append(v_s)
        xp = xp + gp * yp
        xs = xs + gs * ys
    y_prompt = rms_norm(xp, final_g)
    y_sample = rms_norm(xs, final_g)
    dt = x_prompt.dtype

    def st(lst):
        return jnp.stack(lst).astype(dt)

    return (y_prompt, y_sample,
            st(sbk_p), st(sbv_p), st(sbk_s), st(sbv_s),
            st(mlc_p), st(mln_p), st(mlm_p), st(mlc_s), st(mln_s), st(mlm_s),
            st(glas_p), st(glas_s),
            st(mbk_p), st(mbv_p), st(mbk_s), st(mbv_s))
```

```python
import functools

import jax
import jax.numpy as jnp
from jax import lax
from jax.experimental import pallas as pl
from jax.experimental.pallas import tpu as pltpu

F32 = jnp.float32
BF16 = jnp.bfloat16

D_MODEL = 1024
EPS = 1e-6
PAGE = 128
SB_HEADS, SB_DIM = 8, 64
ML_HEADS, ML_DIM, ML_CHUNK = 4, 128, 64
GLA_HEADS, GLA_DK, GLA_DV, GLA_RANK, GLA_TAU = 4, 64, 128, 16, 16.0
MB_HEADS, MB_DIM, MB_BLOCK, MB_TOPK = 8, 64, 256, 3
HEAD_W = 512
PAIR_W = 128
SAMPLE_PAD = 8
NEG = -1e30
SB_SKIP = 100.0
GLA_SUB = 16


def _dot(a, b):
    return jnp.dot(a.astype(BF16), b.astype(BF16), preferred_element_type=F32)


def _dot_nt(a, b):
    return lax.dot_general(a.astype(BF16), b.astype(BF16), (((1,), (1,)), ((), ())), preferred_element_type=F32)


def _dot_tn(a, b):
    return lax.dot_general(a.astype(BF16), b.astype(BF16), (((0,), (0,)), ((), ())), preferred_element_type=F32)


def _split_dot(m01, x):
    hi = x.astype(BF16)
    lo = (x - hi.astype(F32)).astype(BF16)
    return jnp.dot(m01, hi, preferred_element_type=F32) + jnp.dot(m01, lo, preferred_element_type=F32)


def _softplus_tail(z):
    return jnp.log(1.0 + jnp.exp(-jnp.abs(z)))


def _log_sigmoid(z):
    return jnp.minimum(z, 0.0) - _softplus_tail(z)


def _sigmoid(z):
    return 1.0 / (1.0 + jnp.exp(-z))


def _silu(z):
    return z * _sigmoid(z)


def _iota(shape, axis):
    return lax.broadcasted_iota(jnp.int32, shape, axis)


def _params(sem, vmem_mb=None):
    kw = dict(dimension_semantics=sem)
    if vmem_mb is not None:
        kw["vmem_limit_bytes"] = vmem_mb * 1024 * 1024
    return pltpu.CompilerParams(**kw)


def _mod_kernel(c_ref, w_ref, b_ref, o_ref):
    o_ref[...] = _dot(_silu(c_ref[...]), w_ref[...]) + b_ref[...]


def _modulation(c_all, w_mod, b_mod):
    depth = w_mod.shape[0]
    rows = c_all.shape[0]
    return pl.pallas_call(
        _mod_kernel,
        grid=(depth, 3),
        in_specs=[
            pl.BlockSpec((rows, D_MODEL), lambda l, j: (0, 0)),
            pl.BlockSpec((None, D_MODEL, D_MODEL), lambda l, j: (l, 0, j)),
            pl.BlockSpec((None, 1, D_MODEL), lambda l, j: (l, 0, j)),
        ],
        out_specs=pl.BlockSpec((None, rows, D_MODEL), lambda l, j: (l, 0, j)),
        out_shape=jax.ShapeDtypeStruct((depth, rows, 3 * D_MODEL), F32),
        name="modulation",
        compiler_params=_params(("arbitrary", "arbitrary")),
    )(c_all, w_mod, b_mod.reshape(depth, 1, 3 * D_MODEL))


def _inproj_kernel(x_ref, mod_ref, g_ref, w_ref, *out_refs, widths):
    x = x_ref[...]
    y = x * lax.rsqrt(jnp.mean(x * x, axis=-1, keepdims=True) + EPS) * g_ref[...]
    mod = mod_ref[...]
    h = (y * (1.0 + mod[:, D_MODEL:2 * D_MODEL]) + mod[:, :D_MODEL]).astype(BF16)
    off = 0
    for o_ref, wd in zip(out_refs, widths):
        o_ref[...] = jnp.dot(h, w_ref[:, off:off + wd], preferred_element_type=F32)
        off += wd


def _inproj(x, mod, g, w, widths, rows_per_group, tm):
    m = x.shape[0]
    r = mod.shape[1]
    n = w.shape[1]
    return pl.pallas_call(
        functools.partial(_inproj_kernel, widths=widths),
        grid=(m // tm,),
        in_specs=[
            pl.BlockSpec((tm, D_MODEL), lambda i: (i, 0)),
            pl.BlockSpec((None, r, 3 * D_MODEL), lambda i: ((i * tm) // rows_per_group, 0, 0)),
            pl.BlockSpec((1, D_MODEL), lambda i: (0, 0)),
            pl.BlockSpec((D_MODEL, n), lambda i: (0, 0)),
        ],
        out_specs=[pl.BlockSpec((tm, wd), lambda i: (i, 0)) for wd in widths],
        out_shape=[jax.ShapeDtypeStruct((m, wd), F32) for wd in widths],
        name="inproj",
        compiler_params=_params(("arbitrary",), 56),
    )(x, mod, g.reshape(1, D_MODEL), w)


def _outproj_kernel(a_ref, b_ref, x_ref, mod_ref, w_ref, fg_ref, o_ref, *, final):
    y = _dot(a_ref[...], w_ref[:HEAD_W, :]) + _dot(b_ref[...], w_ref[HEAD_W:, :])
    xn = x_ref[...] + mod_ref[...][:, 2 * D_MODEL:] * y
    if final:
        xn = xn * lax.rsqrt(jnp.mean(xn * xn, axis=-1, keepdims=True) + EPS) * fg_ref[...]
    o_ref[...] = xn


def _outproj(a, b, x, mod, w, final_g, rows_per_group, tm, final):
    m = x.shape[0]
    r = mod.shape[1]
    return pl.pallas_call(
        functools.partial(_outproj_kernel, final=final),
        grid=(m // tm,),
        in_specs=[
            pl.BlockSpec((tm, HEAD_W), lambda i: (i, 0)),
            pl.BlockSpec((tm, HEAD_W), lambda i: (i, 0)),
            pl.BlockSpec((tm, D_MODEL), lambda i: (i, 0)),
            pl.BlockSpec((None, r, 3 * D_MODEL), lambda i: ((i * tm) // rows_per_group, 0, 0)),
            pl.BlockSpec((2 * HEAD_W, D_MODEL), lambda i: (0, 0)),
            pl.BlockSpec((1, D_MODEL), lambda i: (0, 0)),
        ],
        out_specs=pl.BlockSpec((tm, D_MODEL), lambda i: (i, 0)),
        out_shape=jax.ShapeDtypeStruct((m, D_MODEL), F32),
        name="outproj",
        compiler_params=_params(("arbitrary",)),
    )(a, b, x, mod, w, final_g.reshape(1, D_MODEL))


def _sb_tile(kt, vt, q_cols, valid, carry, ut):
    zt = q_cols(kt)
    tail = _softplus_tail(zt)
    log1m = jnp.minimum(-zt, 0.0) - tail
    logsig = jnp.minimum(zt, 0.0) - tail
    if valid is not None:
        log1m = jnp.where(valid, log1m, 0.0)
    after = _split_dot(ut, log1m) + carry
    w = jnp.exp(logsig + after)
    if valid is not None:
        w = jnp.where(valid, w, 0.0)
    return w, after[0:1, :] + log1m[0:1, :]


def _strict_upper(n):
    return jnp.where(_iota((n, n), 1) > _iota((n, n), 0), 1.0, 0.0).astype(BF16)


def _sb_prompt_kernel(q_ref, k_ref, v_ref, z_ref, o_ref, acc_ref, *, tq):
    i = pl.program_id(2)
    row = _iota((tq, tq), 0)
    col = _iota((tq, tq), 1)
    ut = _strict_upper(tq)
    q_t = (q_ref[...] * (SB_DIM ** -0.5)).T
    row_head = _iota((PAIR_W, tq), 0) // SB_DIM
    q_tm = [jnp.where(row_head == hh, q_t, 0.0).astype(BF16) for hh in range(2)]
    lane_head = _iota((tq, PAIR_W), 1) // SB_DIM
    acc_ref[...] = jnp.zeros_like(acc_ref)

    def cond(c):
        return (c[0] >= 0) & (c[3] > -SB_SKIP)

    def body(c):
        j, c0, c1, _ = c
        start = pl.multiple_of(j * tq, tq)
        kj = k_ref[pl.ds(start, tq), :].astype(BF16)
        vj = v_ref[pl.ds(start, tq), :]
        valid = (j * tq + row) < (i * tq + col)
        new = []
        upd = jnp.zeros((PAIR_W, tq), F32)
        for hh, ch in enumerate((c0, c1)):
            w, cn = _sb_tile(kj, None, lambda kt, hh=hh: jnp.dot(kt, q_tm[hh], preferred_element_type=F32),
                             valid, ch, ut)
            upd = upd + _dot_tn(jnp.where(lane_head == hh, vj, 0.0), w)
            new.append(cn)
        acc_ref[...] += upd
        return j - 1, new[0], new[1], jnp.max(jnp.maximum(new[0], new[1]))

    zero = jnp.zeros((1, tq), F32)
    lax.while_loop(cond, body, (i, zero, zero, jnp.float32(0.0)))
    o_ref[...] = acc_ref[...].T * _silu(z_ref[...])


def _sb_prompt(q, k, v, z, tq=128):
    b, s, _ = q.shape
    blk_q = pl.BlockSpec((None, tq, PAIR_W), lambda bi, p, i: (bi, i, p))
    blk_kv = pl.BlockSpec((None, s, PAIR_W), lambda bi, p, i: (bi, 0, p))
    return pl.pallas_call(
        functools.partial(_sb_prompt_kernel, tq=tq),
        grid=(b, HEAD_W // PAIR_W, s // tq),
        in_specs=[blk_q, blk_kv, blk_kv, blk_q],
        out_specs=blk_q,
        out_shape=jax.ShapeDtypeStruct((b, s, HEAD_W), F32),
        scratch_shapes=[pltpu.VMEM((PAIR_W, tq), F32)],
        name="sb_prompt",
        compiler_params=_params(("arbitrary", "arbitrary", "arbitrary")),
    )(q, k, v, z)


def _tile_rows(x, reps):
    return jnp.concatenate([x] * reps, axis=0)


def _block_diag_queries(q, n_cols, head_dim):
    qt = _tile_rows(q, n_cols // SAMPLE_PAD)
    row_head = _iota((n_cols, HEAD_W), 0) // SAMPLE_PAD
    lane_head = _iota((n_cols, HEAD_W), 1) // head_dim
    return jnp.where(row_head == lane_head, qt, 0.0)


def _gather_heads(res, head_dim):
    lane_head = _iota((SAMPLE_PAD, HEAD_W), 1) // head_dim
    out = jnp.zeros((SAMPLE_PAD, HEAD_W), F32)
    for h in range(HEAD_W // head_dim):
        out = out + jnp.where(lane_head == h, res[h * SAMPLE_PAD:(h + 1) * SAMPLE_PAD, :], 0.0)
    return out


def _sb_sample_kernel(pt_ref, q_ref, kn_ref, vn_ref, z_ref, ck_hbm, cv_hbm, o_ref,
                      kbuf, vbuf, sem, acc_ref, kpad, vpad, *, n_pages, t_real):
    b = pl.program_id(0)
    nq = PAIR_W

    def copies(p, slot):
        phys = pt_ref[b * n_pages + p]
        return (pltpu.make_async_copy(ck_hbm.at[phys], kbuf.at[slot], sem.at[0, slot]),
                pltpu.make_async_copy(cv_hbm.at[phys], vbuf.at[slot], sem.at[1, slot]))

    def start(p, slot):
        for cp in copies(p, slot):
            cp.start()

    def wait(p, slot):
        for cp in copies(p, slot):
            cp.wait()

    start(n_pages - 1, (n_pages - 1) % 2)
    start(n_pages - 2, (n_pages - 2) % 2)

    @pl.when(b == 0)
    def _():
        kpad[...] = jnp.zeros_like(kpad)
        vpad[...] = jnp.zeros_like(vpad)

    kpad[0:SAMPLE_PAD, :] = kn_ref[...]
    vpad[0:SAMPLE_PAD, :] = vn_ref[...]

    qbd = _block_diag_queries(q_ref[...] * (SB_DIM ** -0.5), nq, SB_DIM).astype(BF16)
    later = jnp.where(_iota((PAGE, PAGE), 0) > _iota((PAGE, PAGE), 1), 1.0, 0.0).astype(BF16)
    qrow = _iota((nq, PAGE), 0)
    key = _iota((nq, PAGE), 1)
    valid_new = (key < qrow % SAMPLE_PAD) & (key < t_real)
    row1 = _iota((nq, 1), 0)
    real_row = (row1 < SB_HEADS * SAMPLE_PAD) & (row1 % SAMPLE_PAD < t_real)

    def tile(z, valid, carry):
        tail = _softplus_tail(z)
        log1m = jnp.minimum(-z, 0.0) - tail
        logsig = jnp.minimum(z, 0.0) - tail
        if valid is not None:
            log1m = jnp.where(valid, log1m, 0.0)
        hi = log1m.astype(BF16)
        lo = (log1m - hi.astype(F32)).astype(BF16)
        after = (jnp.dot(hi, later, preferred_element_type=F32) + jnp.dot(lo, later, preferred_element_type=F32)
                 + carry)
        w = jnp.exp(logsig + after)
        if valid is not None:
            w = jnp.where(valid, w, 0.0)
        return w, after[:, 0:1] + log1m[:, 0:1]

    w, carry = tile(_dot_nt(qbd, kpad[...]), valid_new, jnp.zeros((nq, 1), F32))
    acc_ref[...] = _dot(w, vpad[...])

    def row_max(c):
        return jnp.max(jnp.where(real_row, c, -jnp.inf))

    def cond(c):
        return (c[0] >= 0) & (c[2] > -SB_SKIP)

    def body(c):
        p, carry, _ = c
        slot = p % 2
        wait(p, slot)
        w, cn = tile(jnp.dot(qbd, kbuf[slot].astype(BF16), preferred_element_type=F32), None, carry)
        acc_ref[...] += _dot_nt(w, vbuf[slot])

        @pl.when(p >= 2)
        def _():
            start(p - 2, slot)

        return p - 1, cn, row_max(cn)

    p_end, _, _ = lax.while_loop(cond, body, (jnp.int32(n_pages - 1), carry, row_max(carry)))

    @pl.when(p_end >= 0)
    def _():
        wait(p_end, p_end % 2)

    @pl.when(p_end >= 1)
    def _():
        wait(p_end - 1, (p_end - 1) % 2)

    o_ref[...] = _gather_heads(acc_ref[...], SB_DIM) * _silu(z_ref[...])


def _sb_sample(q, kn, vn, z, cache_k, cache_v, page_table, t_real):
    nb = q.shape[0]
    n_pages = page_table.shape[1]
    blk = pl.BlockSpec((None, SAMPLE_PAD, HEAD_W), lambda bi, pt: (bi, 0, 0))
    hbm = pl.BlockSpec(memory_space=pl.ANY)
    return pl.pallas_call(
        functools.partial(_sb_sample_kernel, n_pages=n_pages, t_real=t_real),
        grid_spec=pltpu.PrefetchScalarGridSpec(
            num_scalar_prefetch=1,
            grid=(nb,),
            in_specs=[blk, blk, blk, blk, hbm, hbm],
            out_specs=blk,
            scratch_shapes=[
                pltpu.VMEM((2, HEAD_W, PAGE), F32),
                pltpu.VMEM((2, HEAD_W, PAGE), F32),
                pltpu.SemaphoreType.DMA((2, 2)),
                pltpu.VMEM((PAIR_W, HEAD_W), F32),
                pltpu.VMEM((PAGE, HEAD_W), F32),
                pltpu.VMEM((PAGE, HEAD_W), F32),
            ],
        ),
        out_shape=jax.ShapeDtypeStruct((nb, SAMPLE_PAD, HEAD_W), F32),
        name="sb_sample",
        compiler_params=_params(("arbitrary",)),
    )(page_table.reshape(-1), q, kn, vn, z, cache_k, cache_v)


def _mlstm_kernel(q_ref, k_ref, v_ref, og_ref, z_ref, gate_ref, bias_ref, gn_ref, c0_ref, n0_ref, m0_ref,
                  out_ref, c_out, n_out, m_out, c_sc, n_sc, m_sc, *, chunk, valid):
    ci = pl.program_id(1)

    @pl.when(ci == 0)
    def _():
        c_sc[...] = c0_ref[...]
        n_sc[...] = n0_ref[...]
        m_sc[...] = m0_ref[...]

    pre = gate_ref[...] + bias_ref[...]
    lf = pltpu.roll(_log_sigmoid(pre), PAIR_W - ML_HEADS, 1)
    ig = pre
    if valid < chunk:
        live = _iota((chunk, PAIR_W), 0) < valid
        lf = jnp.where(live, lf, 0.0)
        ig = jnp.where(live, ig, -jnp.inf)
    r = _iota((chunk, chunk), 0)
    c = _iota((chunk, chunk), 1)
    tril = c <= r
    bc = _split_dot(jnp.where(tril, 1.0, 0.0).astype(BF16), lf)
    a = ig - bc
    a_t = a.T
    for h in range(ML_HEADS):
        sl = slice(h * ML_DIM, (h + 1) * ML_DIM)
        bc_h = bc[:, h:h + 1]
        a_row = a_t[h:h + 1, :]
        m0 = m_sc[h:h + 1, 0:1]
        cm = jnp.max(jnp.where(tril, a_row, -jnp.inf), axis=-1, keepdims=True)
        m_h = jnp.maximum(bc_h + m0, cm + bc_h)
        dmat = jnp.exp(jnp.where(tril, bc_h - m_h + a_row, -jnp.inf))
        qh = q_ref[:, sl]
        kh = k_ref[:, sl] * (ML_DIM ** -0.5)
        vh = v_ref[:, sl]
        wts = dmat * _dot_nt(qh, kh)
        inter = jnp.exp(bc_h + m0 - m_h)
        n_h = n_sc[h:h + 1, :]
        num = inter * _dot(qh, c_sc[h]) + _dot(wts, vh)
        den = inter * jnp.sum(qh * n_h, axis=-1, keepdims=True) + jnp.sum(wts, axis=-1, keepdims=True)
        hv = num / jnp.maximum(jnp.abs(den), jnp.exp(-m_h))
        hn = hv * lax.rsqrt(jnp.mean(hv * hv, axis=-1, keepdims=True) + EPS) * gn_ref[:, sl]
        out_ref[:, sl] = hn * _sigmoid(og_ref[:, sl]) * _silu(z_ref[:, sl])
        bc_l = bc_h[chunk - 1:chunk, :]
        m_l = m_h[chunk - 1:chunk, :]
        decay = jnp.exp(bc_l + m0 - m_l)
        wk = jnp.exp(bc_l + a[:, h:h + 1] - m_l) * kh
        c_sc[h] = decay * c_sc[h] + _dot_tn(wk, vh)
        n_sc[h:h + 1, :] = decay * n_h + jnp.sum(wk, axis=0, keepdims=True)
        m_sc[h:h + 1, :] = jnp.broadcast_to(m_l, (1, PAIR_W))

    @pl.when(ci == pl.num_programs(1) - 1)
    def _():
        c_out[...] = c_sc[...]
        n_out[...] = n_sc[...]
        m_out[...] = m_sc[...]


def _mlstm(q, k, v, og, z, gates, bias, gn, c0, n0, m0, nb, chunk, valid):
    m = q.shape[0]
    n_chunks = m // (nb * chunk)
    tok = pl.BlockSpec((chunk, HEAD_W), lambda b, ci: (b * n_chunks + ci, 0))
    st_c = pl.BlockSpec((None, ML_HEADS, ML_DIM, ML_DIM), lambda b, ci: (b, 0, 0, 0))
    st_v = pl.BlockSpec((None, SAMPLE_PAD, ML_DIM), lambda b, ci: (b, 0, 0))
    return pl.pallas_call(
        functools.partial(_mlstm_kernel, chunk=chunk, valid=valid),
        grid=(nb, n_chunks),
        in_specs=[tok, tok, tok, tok, tok,
                  pl.BlockSpec((chunk, PAIR_W), lambda b, ci: (b * n_chunks + ci, 0)),
                  pl.BlockSpec((1, PAIR_W), lambda b, ci: (0, 0)),
                  pl.BlockSpec((1, HEAD_W), lambda b, ci: (0, 0)),
                  st_c, st_v, st_v],
        out_specs=[tok, st_c, st_v, st_v],
        out_shape=[jax.ShapeDtypeStruct((m, HEAD_W), F32),
                   jax.ShapeDtypeStruct((nb, ML_HEADS, ML_DIM, ML_DIM), F32),
                   jax.ShapeDtypeStruct((nb, SAMPLE_PAD, ML_DIM), F32),
                   jax.ShapeDtypeStruct((nb, SAMPLE_PAD, ML_DIM), F32)],
        scratch_shapes=[pltpu.VMEM((ML_HEADS, ML_DIM, ML_DIM), F32),
                        pltpu.VMEM((SAMPLE_PAD, ML_DIM), F32),
                        pltpu.VMEM((SAMPLE_PAD, ML_DIM), F32)],
        name="mlstm",
        compiler_params=_params(("arbitrary", "arbitrary")),
    )(q, k, v, og, z, gates, bias, gn, c0, n0, m0)


def _gla_kernel(q_ref, k_ref, v_ref, z_ref, gg_ref, wg_ref, bg_ref, gn_ref, s0_ref, out_ref, s_out,
                st_sc, kb, gb, vb, qa_sc, kc_sc, dec_sc, oi_sc, *, tb, sub, valid):
    ti = pl.program_id(1)
    kw = GLA_HEADS * GLA_DK

    @pl.when(ti == 0)
    def _():
        for p in range(2):
            st_sc[p] = s0_ref[p].T
        kb[0:sub, :] = jnp.zeros((sub, kw), F32)
        gb[0:sub, :] = jnp.zeros((sub, kw), F32)
        vb[0:sub, :] = jnp.zeros((sub, HEAD_W), F32)

    lg = _log_sigmoid(_dot(gg_ref[...], wg_ref[...]) + bg_ref[...]) * (1.0 / GLA_TAU)
    k = k_ref[...]
    if valid < tb:
        live = _iota((tb, kw), 0) < valid
        lg = jnp.where(live, lg, 0.0)
        k = jnp.where(live, k, 0.0)
    r = _iota((tb, tb), 0)
    c = _iota((tb, tb), 1)
    same = (r // sub) == (c // sub)
    g = _split_dot(jnp.where(same & (c <= r), 1.0, 0.0).astype(BF16), lg)
    gend = _split_dot(jnp.where(same, 1.0, 0.0).astype(BF16), lg)
    q = q_ref[...] * (GLA_DK ** -0.5)
    v = v_ref[...]
    qa_sc[...] = q * jnp.exp(g)
    kc_sc[...] = k * jnp.exp(gend - g)
    dec_sc[...] = jnp.exp(gend)
    kb[sub:sub + tb, :] = k
    gb[sub:sub + tb, :] = g
    vb[sub:sub + tb, :] = v

    head_sum = jnp.where(_iota((kw, PAIR_W), 0) // GLA_DK == _iota((kw, PAIR_W), 1), 1.0, 0.0).astype(BF16)
    head_bcast = jnp.where(_iota((PAIR_W, HEAD_W), 0) == _iota((PAIR_W, HEAD_W), 1) // GLA_DV, 1.0, 0.0).astype(BF16)
    rmod = _iota((tb, 1), 0) % sub
    oi = jnp.zeros((tb, HEAD_W), F32)
    for d in range(sub):
        lo = sub - d
        dec = jnp.exp(jnp.where(rmod >= d, g - gb[lo:lo + tb, :], NEG))
        att = _dot(q * kb[lo:lo + tb, :] * dec, head_sum)
        oi = oi + _dot(att, head_bcast) * vb[lo:lo + tb, :]
    oi_sc[...] = oi

    lane_head = _iota((sub, PAIR_W), 1) // GLA_DK

    def step(i, _):
        rows = pl.ds(pl.multiple_of(i * sub, sub), sub)
        qa = qa_sc[rows, :]
        kc = kc_sc[rows, :]
        dec = dec_sc[rows, :][0:1, :]
        for p in range(2):
            ps = slice(p * PAIR_W, (p + 1) * PAIR_W)
            st = st_sc[p]
            new = st * dec[:, ps]
            for hh in range(2):
                hs = slice((2 * p + hh) * GLA_DV, (2 * p + hh + 1) * GLA_DV)
                o = _dot_nt(jnp.where(lane_head == hh, qa[:, ps], 0.0), st) + oi_sc[rows, hs]
                on = o * lax.rsqrt(jnp.mean(o * o, axis=-1, keepdims=True) + EPS) * gn_ref[:, hs]
                out_ref[rows, hs] = on * _silu(z_ref[rows, hs])
                new = new + _dot_tn(v_ref[rows, hs], jnp.where(lane_head == hh, kc[:, ps], 0.0))
            st_sc[p] = new
        return 0

    lax.fori_loop(0, tb // sub, step, 0)

    @pl.when(ti == pl.num_programs(1) - 1)
    def _():
        for p in range(2):
            s_out[p] = st_sc[p].T


def _gla(q, k, v, z, gg, wg, bg, gn, s0, nb, tb, sub, valid):
    m = q.shape[0]
    nt = m // (nb * tb)
    kw = GLA_HEADS * GLA_DK
    tok_k = pl.BlockSpec((tb, kw), lambda b, t: (b * nt + t, 0))
    tok_v = pl.BlockSpec((tb, HEAD_W), lambda b, t: (b * nt + t, 0))
    st = pl.BlockSpec((None, 2, PAIR_W, GLA_DV), lambda b, t: (b, 0, 0, 0))
    return pl.pallas_call(
        functools.partial(_gla_kernel, tb=tb, sub=sub, valid=valid),
        grid=(nb, nt),
        in_specs=[tok_k, tok_k, tok_v, tok_v,
                  pl.BlockSpec((tb, PAIR_W), lambda b, t: (b * nt + t, 0)),
                  pl.BlockSpec((PAIR_W, kw), lambda b, t: (0, 0)),
                  pl.BlockSpec((1, kw), lambda b, t: (0, 0)),
                  pl.BlockSpec((1, HEAD_W), lambda b, t: (0, 0)),
                  st],
        out_specs=[tok_v, st],
        out_shape=[jax.ShapeDtypeStruct((m, HEAD_W), F32),
                   jax.ShapeDtypeStruct((nb, 2, PAIR_W, GLA_DV), F32)],
        scratch_shapes=[pltpu.VMEM((2, GLA_DV, PAIR_W), F32),
                        pltpu.VMEM((sub + tb, kw), F32),
                        pltpu.VMEM((sub + tb, kw), F32),
                        pltpu.VMEM((sub + tb, HEAD_W), F32),
                        pltpu.VMEM((tb, kw), F32),
                        pltpu.VMEM((tb, kw), F32),
                        pltpu.VMEM((tb, kw), F32),
                        pltpu.VMEM((tb, HEAD_W), F32)],
        name="gla",
        compiler_params=_params(("arbitrary", "arbitrary")),
    )(q, k, v, z, gg, wg, bg, gn, s0)


def _top3_rows(s, n_valid):
    nb = s.shape[0]
    nrow = _iota(s.shape, 0)
    live = nrow < n_valid
    s = jnp.where(live, s, -jnp.inf)
    chosen = jnp.zeros(s.shape, F32)
    picks = []
    for _ in range(MB_TOPK):
        mx = jnp.max(s, axis=0, keepdims=True)
        idx = jnp.min(jnp.where(s == mx, nrow, nb), axis=0, keepdims=True)
        pick = nrow == idx
        chosen = jnp.where(pick, jnp.where(live, 1.0, 0.0), chosen)
        s = jnp.where(pick, -jnp.inf, s)
        picks.append(idx)
    return picks, chosen


def _moba_prompt_kernel(q_ref, k_ref, v_ref, z_ref, o_ref, km_sc, sel2_sc, sel_sc, acc_sc, m_sc, l_sc, *, tq, nblk):
    p = pl.program_id(1)
    i = pl.program_id(2)

    @pl.when(i == 0)
    def _():
        for n in range(nblk):
            km_sc[n:n + 1, :] = jnp.mean(k_ref[n * MB_BLOCK:(n + 1) * MB_BLOCK, :], axis=0, keepdims=True)

    q = q_ref[...]
    q_t = q.T
    row_head = _iota((PAIR_W, tq), 0) // MB_DIM
    lane_head_km = _iota((nblk, PAIR_W), 1) // MB_DIM
    lane_head = _iota((MB_BLOCK, PAIR_W), 1) // MB_DIM
    km = km_sc[...]
    q_tm = []
    for hh in range(2):
        s = jnp.dot(jnp.where(lane_head_km == hh, km, 0.0), q_t, precision=lax.Precision.HIGHEST,
                    preferred_element_type=F32)
        _, chosen = _top3_rows(s, i)
        sel2_sc[...] = chosen
        for n in range(nblk):
            sel_sc[hh, n] = sel2_sc[n:n + 1, :]
        q_tm.append(jnp.where(row_head == hh, q_t * (MB_DIM ** -0.5), 0.0).astype(BF16))
    acc_sc[...] = jnp.zeros_like(acc_sc)
    m_sc[...] = jnp.full(m_sc.shape, NEG, F32)
    l_sc[...] = jnp.zeros_like(l_sc)
    col_minus_row = (_iota((MB_BLOCK, tq), 1) - _iota((MB_BLOCK, tq), 0)).astype(F32)
    causal = _iota((MB_BLOCK, tq), 0) <= _iota((MB_BLOCK, tq), 1)
    slopes = [jnp.exp2(-(jnp.full((1, tq), 2 * p + hh + 1, jnp.int32)).astype(F32)) for hh in range(2)]

    def tile(n, diag):
        start = pl.multiple_of(n * MB_BLOCK, MB_BLOCK)
        kn = k_ref[pl.ds(start, MB_BLOCK), :].astype(BF16)
        vn = v_ref[pl.ds(start, MB_BLOCK), :]
        dist = col_minus_row + ((i - n) * MB_BLOCK).astype(F32)
        acc = acc_sc[...]
        for hh in range(2):
            s = jnp.dot(kn, q_tm[hh], preferred_element_type=F32) - slopes[hh] * dist
            allowed = causal if diag else (sel_sc[hh, n] > 0.5)
            s = jnp.where(allowed, s, NEG)
            m_old = m_sc[hh]
            m_new = jnp.maximum(m_old, jnp.max(s, axis=0, keepdims=True))
            pr = jnp.exp(s - m_new)
            alpha = jnp.exp(m_old - m_new)
            l_sc[hh] = alpha * l_sc[hh] + jnp.sum(pr, axis=0, keepdims=True)
            m_sc[hh] = m_new
            acc = jnp.where(row_head == hh, acc * alpha, acc) + _dot_tn(jnp.where(lane_head == hh, vn, 0.0), pr)
        acc_sc[...] = acc

    tile(i, True)

    def body(n, _):
        tile(n, False)
        return 0

    lax.fori_loop(0, i, body, 0)
    inv = jnp.where(row_head == 0, 1.0 / l_sc[0], 1.0 / l_sc[1])
    o_ref[...] = (acc_sc[...] * inv).T * _silu(z_ref[...])


def _moba_prompt(q, k, v, z):
    b, s, _ = q.shape
    tq = MB_BLOCK
    nblk = s // MB_BLOCK
    blk_q = pl.BlockSpec((None, tq, PAIR_W), lambda bi, p, i: (bi, i, p))
    blk_kv = pl.BlockSpec((None, s, PAIR_W), lambda bi, p, i: (bi, 0, p))
    return pl.pallas_call(
        functools.partial(_moba_prompt_kernel, tq=tq, nblk=nblk),
        grid=(b, HEAD_W // PAIR_W, s // tq),
        in_specs=[blk_q, blk_kv, blk_kv, blk_q],
        out_specs=blk_q,
        out_shape=jax.ShapeDtypeStruct((b, s, HEAD_W), F32),
        scratch_shapes=[pltpu.VMEM((nblk, PAIR_W), F32),
                        pltpu.VMEM((nblk, tq), F32),
                        pltpu.VMEM((2, nblk, 1, tq), F32),
                        pltpu.VMEM((PAIR_W, tq), F32),
                        pltpu.VMEM((2, 1, tq), F32),
                        pltpu.VMEM((2, 1, tq), F32)],
        name="moba_prompt",
        compiler_params=_params(("arbitrary", "arbitrary", "arbitrary")),
    )(q, k, v, z)


_MEAN_PAGES = 16


def _top3_lanes(s, n_valid):
    width = s.shape[1]
    lane = _iota(s.shape, 1)
    s = jnp.where(lane < n_valid, s, -jnp.inf)
    picks = []
    for _ in range(MB_TOPK):
        mx = jnp.max(s, axis=1, keepdims=True)
        idx = jnp.min(jnp.where(s == mx, lane, width), axis=1, keepdims=True)
        s = jnp.where(lane == idx, -jnp.inf, s)
        picks.append(idx)
    return picks


def _mb_select_kernel(pt_ref, *refs, n_blocks):
    pages = refs[:_MEAN_PAGES]
    q_ref, idx_ref, km_sc = refs[_MEAN_PAGES:]
    s = pl.program_id(1)
    per_blk = MB_BLOCK // PAGE
    blk_per_step = _MEAN_PAGES // per_blk

    @pl.when(s == 0)
    def _():
        km_sc[...] = jnp.zeros_like(km_sc)

    lane = _iota((HEAD_W, PAIR_W), 1)
    km = km_sc[...]
    for j in range(blk_per_step):
        tot = sum(jnp.sum(pages[per_blk * j + e][...], axis=1, keepdims=True) for e in range(per_blk))
        km = jnp.where(lane == s * blk_per_step + j, tot * (1.0 / MB_BLOCK), km)
    km_sc[...] = km

    @pl.when(s == pl.num_programs(1) - 1)
    def _():
        qbd = _block_diag_queries(q_ref[...], PAIR_W, MB_DIM)
        sc = jnp.dot(qbd, km, precision=lax.Precision.HIGHEST, preferred_element_type=F32)
        picks = _top3_lanes(sc, n_blocks)
        lane_o = _iota((PAIR_W, PAIR_W), 1)
        out = jnp.zeros((PAIR_W, PAIR_W), jnp.int32)
        for k, pk in enumerate(picks):
            out = jnp.where(lane_o == k, pk, out)
        idx_ref[...] = out


def _mb_select(q, cache_k, page_table):
    nb = q.shape[0]
    n_pages = page_table.shape[1]
    steps = n_pages // _MEAN_PAGES
    n_blocks = n_pages * PAGE // MB_BLOCK
    assert n_pages % _MEAN_PAGES == 0 and MB_TOPK <= n_blocks <= PAIR_W
    page_specs = [pl.BlockSpec((None, HEAD_W, PAGE), lambda bi, s, pt, j=j: (pt[bi * n_pages + s * _MEAN_PAGES + j], 0, 0))
                  for j in range(_MEAN_PAGES)]
    return pl.pallas_call(
        functools.partial(_mb_select_kernel, n_blocks=n_blocks),
        grid_spec=pltpu.PrefetchScalarGridSpec(
            num_scalar_prefetch=1,
            grid=(nb, steps),
            in_specs=page_specs + [pl.BlockSpec((None, SAMPLE_PAD, HEAD_W), lambda bi, s, pt: (bi, 0, 0))],
            out_specs=pl.BlockSpec((None, PAIR_W, PAIR_W), lambda bi, s, pt: (bi, 0, 0)),
            scratch_shapes=[pltpu.VMEM((HEAD_W, PAIR_W), F32)],
        ),
        out_shape=jax.ShapeDtypeStruct((nb, PAIR_W, PAIR_W), jnp.int32),
        name="moba_select",
        compiler_params=_params(("arbitrary", "arbitrary")),
    )(page_table.reshape(-1), *([cache_k] * _MEAN_PAGES), q)


def _mb_sample_kernel(pt_ref, sel_ref, q_ref, kn_ref, vn_ref, z_ref, idx_ref, ck_hbm, cv_hbm, o_ref,
                      kbuf, vbuf, sem, kpad, vpad, s_sc, *, n_pages, t_real):
    b = pl.program_id(0)
    per_blk = MB_BLOCK // PAGE
    n_grp = MB_TOPK * t_real
    n_sel = n_grp * MB_BLOCK
    past = n_pages * PAGE

    def copies(h, k, t, j):
        blk = sel_ref[(b * MB_HEADS * t_real + h * t_real + t) * MB_TOPK + k]
        phys = pt_ref[b * n_pages + blk * per_blk + j]
        hs = pl.ds(h * MB_DIM, MB_DIM)
        ls = pl.ds(((k * t_real + t) * per_blk + j) * PAGE, PAGE)
        return (pltpu.make_async_copy(ck_hbm.at[phys, hs, :], kbuf.at[hs, ls], sem.at[0]),
                pltpu.make_async_copy(cv_hbm.at[phys, hs, :], vbuf.at[hs, ls], sem.at[1]))

    todo = [(h, k, t, j) for h in range(MB_HEADS) for k in range(MB_TOPK) for t in range(t_real)
            for j in range(per_blk)]
    for a in todo:
        for cp in copies(*a):
            cp.start()

    @pl.when(b == 0)
    def _():
        kpad[...] = jnp.zeros_like(kpad)
        vpad[...] = jnp.zeros_like(vpad)

    kpad[0:SAMPLE_PAD, :] = kn_ref[...]
    vpad[0:SAMPLE_PAD, :] = vn_ref[...]

    nq = PAIR_W
    qbd = _block_diag_queries(q_ref[...] * (MB_DIM ** -0.5), nq, MB_DIM).astype(BF16)
    row = _iota((nq, 1), 0)
    t_row = row % SAMPLE_PAD
    slope = jnp.exp2(-(row // SAMPLE_PAD + 1).astype(F32))

    key = _iota((nq, PAGE), 1)
    s_new = _dot_nt(qbd, kpad[...]) - slope * (t_row - key).astype(F32)
    s_new = jnp.where((key <= t_row) & (key < t_real), s_new, NEG)
    s_sc[:, n_sel:n_sel + PAGE] = s_new
    m = jnp.max(s_new, axis=1, keepdims=True)

    for a in todo:
        for cp in copies(*a):
            cp.wait()

    key_in_blk = _iota((nq, MB_BLOCK), 1)
    for g in range(n_grp):
        k, t = divmod(g, t_real)
        lanes = slice(g * MB_BLOCK, (g + 1) * MB_BLOCK)
        pos = idx_ref[:, k:k + 1] * MB_BLOCK + key_in_blk
        sg = _dot(qbd, kbuf[:, lanes]) - slope * (past + t_row - pos).astype(F32)
        sg = jnp.where(t_row == t, sg, NEG)
        s_sc[:, lanes] = sg
        m = jnp.maximum(m, jnp.max(sg, axis=1, keepdims=True))

    pr = jnp.exp(s_sc[...] - m)
    num = _dot_nt(pr[:, :n_sel], vbuf[...]) + _dot(pr[:, n_sel:], vpad[...])
    res = num / jnp.sum(pr, axis=1, keepdims=True)
    o_ref[...] = _gather_heads(res, MB_DIM) * _silu(z_ref[...])


def _mb_sample(q, kn, vn, z, idx, cache_k, cache_v, page_table, t_real):
    nb = q.shape[0]
    n_pages = page_table.shape[1]
    n_sel = MB_TOPK * t_real * MB_BLOCK
    sel = idx[:, :MB_HEADS * SAMPLE_PAD, :MB_TOPK].reshape(nb, MB_HEADS, SAMPLE_PAD, MB_TOPK)[:, :, :t_real]
    blk = pl.BlockSpec((None, SAMPLE_PAD, HEAD_W), lambda bi, pt, sl: (bi, 0, 0))
    hbm = pl.BlockSpec(memory_space=pl.ANY)
    return pl.pallas_call(
        functools.partial(_mb_sample_kernel, n_pages=n_pages, t_real=t_real),
        grid_spec=pltpu.PrefetchScalarGridSpec(
            num_scalar_prefetch=2,
            grid=(nb,),
            in_specs=[blk, blk, blk, blk,
                      pl.BlockSpec((None, PAIR_W, PAIR_W), lambda bi, pt, sl: (bi, 0, 0)),
                      hbm, hbm],
            out_specs=blk,
            scratch_shapes=[pltpu.VMEM((HEAD_W, n_sel), F32),
                            pltpu.VMEM((HEAD_W, n_sel), F32),
                            pltpu.SemaphoreType.DMA((2,)),
                            pltpu.VMEM((PAGE, HEAD_W), F32),
                            pltpu.VMEM((PAGE, HEAD_W), F32),
                            pltpu.VMEM((PAIR_W, n_sel + PAGE), F32)],
        ),
        out_shape=jax.ShapeDtypeStruct((nb, SAMPLE_PAD, HEAD_W), F32),
        name="moba_sample",
        compiler_params=_params(("arbitrary",), 48),
    )(page_table.reshape(-1), sel.reshape(-1), q, kn, vn, z, idx, cache_k, cache_v)


def _pad_cols(w, width):
    return jnp.pad(w, ((0, 0), (0, width - w.shape[1])))


def _pages_feature_major(cache):
    n_pool, page, heads, dim = cache.shape
    return jnp.transpose(cache, (0, 2, 3, 1)).reshape(n_pool, heads * dim, page)


def kernel(x_prompt, x_sample, cache_sb_k, cache_sb_v, cache_mb_k, cache_mb_v, state_ml_C, state_ml_n, state_ml_m, state_gla_S, page_table, c_prompt, c_sample, w_mod, b_mod, norm_g, final_g, w_in_even, w_out_even, ml_b_i, ml_b_f, ml_gn, w_in_odd, w_out_odd, gla_w_g2, gla_b_g, gla_gn):
    bsz, seq, _ = x_prompt.shape
    db, t_real, _ = x_sample.shape
    mp = bsz * seq
    ms = db * SAMPLE_PAD
    n_pool = cache_sb_k.shape[1]

    rows = -(-(bsz + db) // 8) * 8
    c_all = jnp.pad(jnp.concatenate([c_prompt, c_sample], axis=0), ((0, rows - bsz - db), (0, 0)))
    mod = _modulation(c_all, w_mod, b_mod)
    mod_p = mod[:, :bsz, None, :]
    mod_s = jnp.repeat(mod[:, bsz:bsz + db], SAMPLE_PAD, axis=1)[:, None]

    xp = x_prompt.reshape(mp, D_MODEL)
    xs = jnp.pad(x_sample, ((0, 0), (0, SAMPLE_PAD - t_real), (0, 0))).reshape(ms, D_MODEL)
    tm = 256

    w_even = jnp.concatenate([w_in_even[0, :, :9 * HEAD_W], _pad_cols(w_in_even[0, :, 9 * HEAD_W:], PAIR_W)],
                             axis=1).astype(BF16)
    widths_even = (HEAD_W,) * 9 + (PAIR_W,)
    gate_bias = _pad_cols(jnp.concatenate([ml_b_i[0], ml_b_f[0]])[None, :], PAIR_W)
    ml_gn0 = ml_gn[0].reshape(1, HEAD_W)
    w_out0 = w_out_even[0].astype(BF16)

    sq, sk, sv, sz, mq, mk, mv, mo, mz, mg = _inproj(xp, mod_p[0], norm_g[0], w_even, widths_even, seq, tm)
    a_p = _sb_prompt(sq.reshape(bsz, seq, HEAD_W), sk.reshape(bsz, seq, HEAD_W), sv.reshape(bsz, seq, HEAD_W),
                     sz.reshape(bsz, seq, HEAD_W)).reshape(mp, HEAD_W)
    zc = jnp.zeros((bsz, ML_HEADS, ML_DIM, ML_DIM), F32)
    zv = jnp.zeros((bsz, SAMPLE_PAD, ML_DIM), F32)
    b_p, mlc_p, mln_p, mlm_p = _mlstm(mq, mk, mv, mo, mz, mg, gate_bias, ml_gn0, zc, zv, zv, bsz, ML_CHUNK, ML_CHUNK)
    x1p = _outproj(a_p, b_p, xp, mod_p[0], w_out0, final_g, seq, tm, False)
    sbk_p, sbv_p = sk, sv

    sq, sk, sv, sz, mq, mk, mv, mo, mz, mg = _inproj(xs, mod_s[0], norm_g[0], w_even, widths_even, ms, ms)
    r3 = lambda u: u.reshape(db, SAMPLE_PAD, HEAD_W)
    a_s = _sb_sample(r3(sq), r3(sk), r3(sv), r3(sz), _pages_feature_major(cache_sb_k[0]),
                     _pages_feature_major(cache_sb_v[0]), page_table, t_real).reshape(ms, HEAD_W)
    n0 = jnp.pad(state_ml_n[0], ((0, 0), (0, SAMPLE_PAD - ML_HEADS), (0, 0)))
    m0 = jnp.broadcast_to(jnp.pad(state_ml_m[0], ((0, 0), (0, SAMPLE_PAD - ML_HEADS)))[:, :, None],
                          (db, SAMPLE_PAD, ML_DIM))
    b_s, mlc_s, mln_s, mlm_s = _mlstm(mq, mk, mv, mo, mz, mg, gate_bias, ml_gn0, state_ml_C[0], n0, m0,
                                      db, SAMPLE_PAD, t_real)
    x1s = _outproj(a_s, b_s, xs, mod_s[0], w_out0, final_g, ms, ms, False)
    sbk_s, sbv_s = r3(sk)[:, :t_real], r3(sv)[:, :t_real]

    kw = GLA_HEADS * GLA_DK
    g0 = 2 * kw + 2 * HEAD_W
    w_odd = jnp.concatenate([w_in_odd[0, :, :g0], w_in_odd[0, :, g0 + GLA_RANK:],
                             _pad_cols(w_in_odd[0, :, g0:g0 + GLA_RANK], PAIR_W)], axis=1).astype(BF16)
    widths_odd = (kw, kw) + (HEAD_W,) * 6 + (PAIR_W,)
    wg = jnp.pad(gla_w_g2[0], ((0, PAIR_W - GLA_RANK), (0, 0))).astype(BF16)
    bg = gla_b_g[0].reshape(1, kw)
    gla_gn0 = gla_gn[0].reshape(1, HEAD_W)
    w_out1 = w_out_odd[0].astype(BF16)

    gq, gk, gv, gz, dq, dk, dv, dz, gg = _inproj(x1p, mod_p[1], norm_g[1], w_odd, widths_odd, seq, tm)
    s_zero = jnp.zeros((bsz, 2, PAIR_W, GLA_DV), F32)
    c_p, glas_p = _gla(gq, gk, gv, gz, gg, wg, bg, gla_gn0, s_zero, bsz, 256, GLA_SUB, 256)
    rp = lambda u: u.reshape(bsz, seq, HEAD_W)
    d_p = _moba_prompt(rp(dq), rp(dk), rp(dv), rp(dz)).reshape(mp, HEAD_W)
    y_p = _outproj(c_p, d_p, x1p, mod_p[1], w_out1, final_g, seq, tm, True)
    mbk_p, mbv_p = dk, dv

    gq, gk, gv, gz, dq, dk, dv, dz, gg = _inproj(x1s, mod_s[1], norm_g[1], w_odd, widths_odd, ms, ms)
    c_s, glas_s = _gla(gq, gk, gv, gz, gg, wg, bg, gla_gn0, state_gla_S[0].reshape(db, 2, PAIR_W, GLA_DV),
                       db, SAMPLE_PAD, SAMPLE_PAD, t_real)
    ck = _pages_feature_major(cache_mb_k[0])
    cv = _pages_feature_major(cache_mb_v[0])
    idx = _mb_select(r3(dq), ck, page_table)
    d_s = _mb_sample(r3(dq), r3(dk), r3(dv), r3(dz), idx, ck, cv, page_table, t_real).reshape(ms, HEAD_W)
    y_s = _outproj(c_s, d_s, x1s, mod_s[1], w_out1, final_g, ms, ms, True)
    mbk_s, mbv_s = r3(dk)[:, :t_real], r3(dv)[:, :t_real]

    kv_p = lambda u, h, d: u.reshape(1, bsz, seq, h, d)
    kv_s = lambda u, h, d: u.reshape(1, db, t_real, h, d)
    return (y_p.reshape(bsz, seq, D_MODEL), y_s.reshape(db, SAMPLE_PAD, D_MODEL)[:, :t_real],
            kv_p(sbk_p, SB_HEADS, SB_DIM), kv_p(sbv_p, SB_HEADS, SB_DIM),
            kv_s(sbk_s, SB_HEADS, SB_DIM), kv_s(sbv_s, SB_HEADS, SB_DIM),
            mlc_p[None], mln_p[None, :, :ML_HEADS], mlm_p[None, :, :ML_HEADS, 0],
            mlc_s[None], mln_s[None, :, :ML_HEADS], mlm_s[None, :, :ML_HEADS, 0],
            glas_p.reshape(1, bsz, GLA_HEADS, GLA_DK, GLA_DV), glas_s.reshape(1, db, GLA_HEADS, GLA_DK, GLA_DV),
            kv_p(mbk_p, MB_HEADS, MB_DIM), kv_p(mbv_p, MB_HEADS, MB_DIM),
            kv_s(mbk_s, MB_HEADS, MB_DIM), kv_s(mbv_s, MB_HEADS, MB_DIM))
```

```python
import functools

import jax
import jax.numpy as jnp
from jax import lax
from jax.experimental import pallas as pl
from jax.experimental.pallas import tpu as pltpu

F32 = jnp.float32
BF16 = jnp.bfloat16

D_MODEL = 1024
EPS = 1e-6
PAGE = 128
SB_HEADS, SB_DIM = 8, 64
ML_HEADS, ML_DIM, ML_CHUNK = 4, 128, 64
GLA_HEADS, GLA_DK, GLA_DV, GLA_RANK, GLA_TAU = 4, 64, 128, 16, 16.0
MB_HEADS, MB_DIM, MB_BLOCK, MB_TOPK = 8, 64, 256, 3
HEAD_W = 512
PAIR_W = 128
SAMPLE_PAD = 8
NEG = -1e30
LOG2E = 1.4426950408889634
SB_SKIP = 88.0
SB_SKIP2 = SB_SKIP * 1.4426950408889634
GLA_SUB = 16


def _dot(a, b):
    return jnp.dot(a.astype(BF16), b.astype(BF16), preferred_element_type=F32)


def _dot_nt(a, b):
    return lax.dot_general(a.astype(BF16), b.astype(BF16), (((1,), (1,)), ((), ())), preferred_element_type=F32)


def _dot_tn(a, b):
    return lax.dot_general(a.astype(BF16), b.astype(BF16), (((0,), (0,)), ((), ())), preferred_element_type=F32)


def _softplus_tail(z):
    return jnp.log(1.0 + jnp.exp(-jnp.abs(z)))


def _log_sigmoid(z):
    return jnp.minimum(z, 0.0) - _softplus_tail(z)


def _sigmoid(z):
    return 1.0 / (1.0 + jnp.exp(-z))


def _silu(z):
    return z * _sigmoid(z)


def _iota(shape, axis):
    return lax.broadcasted_iota(jnp.int32, shape, axis)


def _params(sem, vmem_mb=None):
    kw = dict(dimension_semantics=sem)
    if vmem_mb is not None:
        kw["vmem_limit_bytes"] = vmem_mb * 1024 * 1024
    return pltpu.CompilerParams(**kw)


def _mod_kernel(c_ref, w_ref, b_ref, o_ref):
    o_ref[...] = _dot(_silu(c_ref[...]), w_ref[...]) + b_ref[...]


def _modulation(c_all, w_mod, b_mod):
    depth = w_mod.shape[0]
    rows = c_all.shape[0]
    return pl.pallas_call(
        _mod_kernel,
        grid=(depth, 3),
        in_specs=[
            pl.BlockSpec((rows, D_MODEL), lambda l, j: (0, 0)),
            pl.BlockSpec((None, D_MODEL, D_MODEL), lambda l, j: (l, 0, j)),
            pl.BlockSpec((None, 1, D_MODEL), lambda l, j: (l, 0, j)),
        ],
        out_specs=pl.BlockSpec((None, rows, D_MODEL), lambda l, j: (l, 0, j)),
        out_shape=jax.ShapeDtypeStruct((depth, rows, 3 * D_MODEL), F32),
        name="modulation",
        compiler_params=_params(("arbitrary", "arbitrary")),
    )(c_all, w_mod, b_mod.reshape(depth, 1, 3 * D_MODEL))


def _inproj_kernel(x_ref, mod_ref, g_ref, w_ref, *refs, widths, widths_t):
    x = x_ref[...]
    y = x * lax.rsqrt(jnp.mean(x * x, axis=-1, keepdims=True) + EPS) * g_ref[...]
    mod = mod_ref[...]
    h = y * (1.0 + mod[:, D_MODEL:2 * D_MODEL]) + mod[:, :D_MODEL]
    hb = h.astype(BF16)
    out_refs = refs[1:] if widths_t else refs
    off = 0
    for o_ref, wd in zip(out_refs, widths):
        o_ref[...] = jnp.dot(hb, w_ref[:, off:off + wd], preferred_element_type=F32)
        off += wd
    if widths_t:
        wt_ref = refs[0]
        ht = h.T.astype(BF16)
        off = 0
        for o_ref, wd in zip(out_refs[len(widths):], widths_t):
            o_ref[...] = jnp.dot(wt_ref[off:off + wd, :], ht, preferred_element_type=F32)
            off += wd


def _inproj(x, mod, g, w, widths, rows_per_group, tm, wt=None, widths_t=()):
    m = x.shape[0]
    r = mod.shape[1]
    n = w.shape[1]
    groups = m // rows_per_group
    per_group = rows_per_group // tm
    in_specs = [
        pl.BlockSpec((tm, D_MODEL), lambda i: (i, 0)),
        pl.BlockSpec((None, r, 3 * D_MODEL), lambda i: ((i * tm) // rows_per_group, 0, 0)),
        pl.BlockSpec((1, D_MODEL), lambda i: (0, 0)),
        pl.BlockSpec((D_MODEL, n), lambda i: (0, 0)),
    ]
    args = [x, mod, g.reshape(1, D_MODEL), w]
    if widths_t:
        in_specs.append(pl.BlockSpec(wt.shape, lambda i: (0, 0)))
        args.append(wt)
    return pl.pallas_call(
        functools.partial(_inproj_kernel, widths=widths, widths_t=widths_t),
        grid=(m // tm,),
        in_specs=in_specs,
        out_specs=([pl.BlockSpec((tm, wd), lambda i: (i, 0)) for wd in widths]
                   + [pl.BlockSpec((None, wd, tm), lambda i: (i // per_group, 0, i % per_group)) for wd in widths_t]),
        out_shape=([jax.ShapeDtypeStruct((m, wd), F32) for wd in widths]
                   + [jax.ShapeDtypeStruct((groups, wd, rows_per_group), F32) for wd in widths_t]),
        name="inproj",
        compiler_params=_params(("arbitrary",), 56),
    )(*args)


def _outproj_kernel(a_ref, b_ref, x_ref, mod_ref, w_ref, fg_ref, o_ref, *, final):
    y = _dot(a_ref[...], w_ref[:HEAD_W, :]) + _dot(b_ref[...], w_ref[HEAD_W:, :])
    xn = x_ref[...] + mod_ref[...][:, 2 * D_MODEL:] * y
    if final:
        xn = xn * lax.rsqrt(jnp.mean(xn * xn, axis=-1, keepdims=True) + EPS) * fg_ref[...]
    o_ref[...] = xn


def _outproj(a, b, x, mod, w, final_g, rows_per_group, tm, final):
    m = x.shape[0]
    r = mod.shape[1]
    return pl.pallas_call(
        functools.partial(_outproj_kernel, final=final),
        grid=(m // tm,),
        in_specs=[
            pl.BlockSpec((tm, HEAD_W), lambda i: (i, 0)),
            pl.BlockSpec((tm, HEAD_W), lambda i: (i, 0)),
            pl.BlockSpec((tm, D_MODEL), lambda i: (i, 0)),
            pl.BlockSpec((None, r, 3 * D_MODEL), lambda i: ((i * tm) // rows_per_group, 0, 0)),
            pl.BlockSpec((2 * HEAD_W, D_MODEL), lambda i: (0, 0)),
            pl.BlockSpec((1, D_MODEL), lambda i: (0, 0)),
        ],
        out_specs=pl.BlockSpec((tm, D_MODEL), lambda i: (i, 0)),
        out_shape=jax.ShapeDtypeStruct((m, D_MODEL), F32),
        name="outproj",
        compiler_params=_params(("arbitrary",)),
    )(a, b, x, mod, w, final_g.reshape(1, D_MODEL))


def _later_matrix(n):
    return jnp.where(_iota((n, n), 0) > _iota((n, n), 1), 1.0, 0.0).astype(BF16)


def _sb_rows_tile(z2, valid, carry, later):
    sp = jnp.maximum(z2, 0.0) + jnp.log2(1.0 + jnp.exp2(-jnp.abs(z2)))
    if valid is not None:
        sp = jnp.where(valid, sp, 0.0)
    hi = sp.astype(BF16)
    lo = (sp - hi.astype(F32)).astype(BF16)
    t = sp + jnp.dot(hi, later, preferred_element_type=F32) + jnp.dot(lo, later, preferred_element_type=F32) + carry
    w = jnp.exp2(z2 - t)
    if valid is not None:
        w = jnp.where(valid, w, 0.0)
    return w, t[:, 0:1]


def _sb_prompt_kernel(q_ref, z_ref, kt_hbm, vt_hbm, o_ref, kt_sc, vt_sc, sem, acc_sc,
                      qst_sc, z_sc, sp_sc, hilo_sc, carry_sc, *, tq):
    b = pl.program_id(0)
    i = pl.program_id(1)

    @pl.when(i == 0)
    def _():
        ck = pltpu.make_async_copy(kt_hbm.at[b], kt_sc, sem.at[0])
        cv = pltpu.make_async_copy(vt_hbm.at[b], vt_sc, sem.at[1])
        ck.start()
        cv.start()
        ck.wait()
        cv.wait()

    n_pairs = SB_HEADS // 2
    rows_all = SB_HEADS * tq
    later = _later_matrix(tq)
    lane_head = _iota((tq, PAIR_W), 1) // SB_DIM
    row_head = _iota((PAIR_W, tq), 0) // SB_DIM
    q = q_ref[...] * (SB_DIM ** -0.5 * LOG2E)
    for p in range(n_pairs):
        qp = q[:, p * PAIR_W:(p + 1) * PAIR_W]
        for hh in range(2):
            qst_sc[p, hh * tq:(hh + 1) * tq, :] = jnp.where(lane_head == hh, qp, 0.0).astype(BF16)
    strictly_before = _iota((2 * tq, tq), 1) < _iota((2 * tq, tq), 0) % tq

    def tile(j, diag):
        start = pl.multiple_of(j * tq, tq)
        for p in range(n_pairs):
            rs = slice(p * 2 * tq, (p + 1) * 2 * tq)
            ktp = kt_sc[p * PAIR_W:(p + 1) * PAIR_W, pl.ds(start, tq)].astype(BF16)
            z2 = jnp.dot(qst_sc[p], ktp, preferred_element_type=F32)
            sp = jnp.maximum(z2, 0.0) + jnp.log2(1.0 + jnp.exp2(-jnp.abs(z2)))
            if diag:
                sp = jnp.where(strictly_before, sp, 0.0)
            hi = sp.astype(BF16)
            z_sc[rs, :] = z2
            sp_sc[rs, :] = sp
            hilo_sc[rs, :] = hi
            hilo_sc[p * 2 * tq + rows_all:(p + 1) * 2 * tq + rows_all, :] = (sp - hi.astype(F32)).astype(BF16)
        cum = jnp.dot(hilo_sc[...], later, preferred_element_type=F32)
        for p in range(n_pairs):
            rs = slice(p * 2 * tq, (p + 1) * 2 * tq)
            t = sp_sc[rs, :] + cum[rs] + cum[p * 2 * tq + rows_all:(p + 1) * 2 * tq + rows_all]
            if not diag:
                t = t + carry_sc[rs, :]
            w = jnp.exp2(z_sc[rs, :] - t)
            if diag:
                w = jnp.where(strictly_before, w, 0.0)
            carry_sc[rs, :] = t[:, 0:1]
            vtp = vt_sc[p * PAIR_W:(p + 1) * PAIR_W, pl.ds(start, tq)]
            w2 = jnp.concatenate([w[0:tq], w[tq:2 * tq]], axis=1)
            v2 = jnp.concatenate([jnp.where(row_head == hh, vtp, 0.0) for hh in range(2)], axis=1)
            upd = _dot_nt(w2, v2)
            if diag:
                acc_sc[:, p * PAIR_W:(p + 1) * PAIR_W] = upd
            else:
                acc_sc[:, p * PAIR_W:(p + 1) * PAIR_W] += upd
        return jnp.min(carry_sc[...])

    least = tile(i, True)

    def cond(c):
        return (c[0] >= 0) & (c[1] < SB_SKIP2)

    def body(c):
        return c[0] - 1, tile(c[0], False)

    lax.while_loop(cond, body, (i - 1, least))
    o_ref[...] = acc_sc[...] * _silu(z_ref[...])


def _sb_prompt(q, kt, vt, z, tq=128):
    b, s, _ = q.shape
    blk = pl.BlockSpec((None, tq, HEAD_W), lambda bi, i: (bi, i, 0))
    hbm = pl.BlockSpec(memory_space=pl.ANY)
    return pl.pallas_call(
        functools.partial(_sb_prompt_kernel, tq=tq),
        grid=(b, s // tq),
        in_specs=[blk, blk, hbm, hbm],
        out_specs=blk,
        out_shape=jax.ShapeDtypeStruct((b, s, HEAD_W), F32),
        scratch_shapes=[pltpu.VMEM((HEAD_W, s), F32),
                        pltpu.VMEM((HEAD_W, s), F32),
                        pltpu.SemaphoreType.DMA((2,)),
                        pltpu.VMEM((tq, HEAD_W), F32),
                        pltpu.VMEM((SB_HEADS // 2, 2 * tq, PAIR_W), BF16),
                        pltpu.VMEM((SB_HEADS * tq, tq), F32),
                        pltpu.VMEM((SB_HEADS * tq, tq), F32),
                        pltpu.VMEM((2 * SB_HEADS * tq, tq), BF16),
                        pltpu.VMEM((SB_HEADS * tq, 1), F32)],
        name="sb_prompt",
        compiler_params=_params(("arbitrary", "arbitrary"), 48),
    )(q, z, kt, vt)


def _tile_rows(x, reps):
    return jnp.concatenate([x] * reps, axis=0)


def _block_diag_queries(q, n_cols, head_dim):
    qt = _tile_rows(q, n_cols // SAMPLE_PAD)
    row_head = _iota((n_cols, HEAD_W), 0) // SAMPLE_PAD
    lane_head = _iota((n_cols, HEAD_W), 1) // head_dim
    return jnp.where(row_head == lane_head, qt, 0.0)


def _gather_heads(res, head_dim):
    lane_head = _iota((SAMPLE_PAD, HEAD_W), 1) // head_dim
    out = jnp.zeros((SAMPLE_PAD, HEAD_W), F32)
    for h in range(HEAD_W // head_dim):
        out = out + jnp.where(lane_head == h, res[h * SAMPLE_PAD:(h + 1) * SAMPLE_PAD, :], 0.0)
    return out


def _sb_sample_kernel(pt_ref, q_ref, kn_ref, vn_ref, z_ref, ck_hbm, cv_hbm, o_ref,
                      kbuf, vbuf, sem, acc_ref, kpad, vpad, *, n_pages, t_real):
    b = pl.program_id(0)
    nq = PAIR_W

    def copies(p, slot):
        phys = pt_ref[b * n_pages + p]
        return (pltpu.make_async_copy(ck_hbm.at[phys], kbuf.at[slot], sem.at[0, slot]),
                pltpu.make_async_copy(cv_hbm.at[phys], vbuf.at[slot], sem.at[1, slot]))

    def start(p, slot):
        for cp in copies(p, slot):
            cp.start()

    def wait(p, slot):
        for cp in copies(p, slot):
            cp.wait()

    start(n_pages - 1, (n_pages - 1) % 2)
    start(n_pages - 2, (n_pages - 2) % 2)

    @pl.when(b == 0)
    def _():
        kpad[...] = jnp.zeros_like(kpad)
        vpad[...] = jnp.zeros_like(vpad)

    kpad[0:SAMPLE_PAD, :] = kn_ref[...]
    vpad[0:SAMPLE_PAD, :] = vn_ref[...]

    qbd = _block_diag_queries(q_ref[...] * (SB_DIM ** -0.5 * LOG2E), nq, SB_DIM).astype(BF16)
    later = _later_matrix(PAGE)
    qrow = _iota((nq, PAGE), 0)
    key = _iota((nq, PAGE), 1)
    valid_new = (key < qrow % SAMPLE_PAD) & (key < t_real)
    row1 = _iota((nq, 1), 0)
    real_row = (row1 < SB_HEADS * SAMPLE_PAD) & (row1 % SAMPLE_PAD < t_real)

    w, carry = _sb_rows_tile(_dot_nt(qbd, kpad[...]), valid_new, jnp.zeros((nq, 1), F32), later)
    acc_ref[...] = _dot(w, vpad[...])

    def least(c):
        return jnp.min(jnp.where(real_row, c, jnp.inf))

    def cond(c):
        return (c[0] >= 0) & (c[2] < SB_SKIP2)

    def body(c):
        p, carry, _ = c
        slot = p % 2
        wait(p, slot)
        w, cn = _sb_rows_tile(jnp.dot(qbd, kbuf[slot].astype(BF16), preferred_element_type=F32), None, carry, later)
        acc_ref[...] += _dot_nt(w, vbuf[slot])

        @pl.when(p >= 2)
        def _():
            start(p - 2, slot)

        return p - 1, cn, least(cn)

    p_end, _, _ = lax.while_loop(cond, body, (jnp.int32(n_pages - 1), carry, least(carry)))

    @pl.when(p_end >= 0)
    def _():
        wait(p_end, p_end % 2)

    @pl.when(p_end >= 1)
    def _():
        wait(p_end - 1, (p_end - 1) % 2)

    o_ref[...] = _gather_heads(acc_ref[...], SB_DIM) * _silu(z_ref[...])


def _sb_sample(q, kn, vn, z, cache_k, cache_v, page_table, t_real):
    nb = q.shape[0]
    n_pages = page_table.shape[1]
    blk = pl.BlockSpec((None, SAMPLE_PAD, HEAD_W), lambda bi, pt: (bi, 0, 0))
    hbm = pl.BlockSpec(memory_space=pl.ANY)
    return pl.pallas_call(
        functools.partial(_sb_sample_kernel, n_pages=n_pages, t_real=t_real),
        grid_spec=pltpu.PrefetchScalarGridSpec(
            num_scalar_prefetch=1,
            grid=(nb,),
            in_specs=[blk, blk, blk, blk, hbm, hbm],
            out_specs=blk,
            scratch_shapes=[
                pltpu.VMEM((2, HEAD_W, PAGE), F32),
                pltpu.VMEM((2, HEAD_W, PAGE), F32),
                pltpu.SemaphoreType.DMA((2, 2)),
                pltpu.VMEM((PAIR_W, HEAD_W), F32),
                pltpu.VMEM((PAGE, HEAD_W), F32),
                pltpu.VMEM((PAGE, HEAD_W), F32),
            ],
        ),
        out_shape=jax.ShapeDtypeStruct((nb, SAMPLE_PAD, HEAD_W), F32),
        name="sb_sample",
        compiler_params=_params(("arbitrary",)),
    )(page_table.reshape(-1), q, kn, vn, z, cache_k, cache_v)


def _mlstm_kernel(q_ref, k_ref, v_ref, og_ref, z_ref, gate_ref, bias_ref, gn_ref, c0_ref, n0_ref, m0_ref,
                  out_ref, c_out, n_out, m_out, c_sc, n_sc, m_sc, *, chunk, valid):
    ci = pl.program_id(1)

    @pl.when(ci == 0)
    def _():
        c_sc[...] = c0_ref[...]
        n_sc[...] = n0_ref[...]
        m_sc[...] = m0_ref[...]

    pre = gate_ref[...] + bias_ref[...]
    lf = pltpu.roll(_log_sigmoid(pre), PAIR_W - ML_HEADS, 1)
    ig = pre
    if valid < chunk:
        live = _iota((chunk, PAIR_W), 0) < valid
        lf = jnp.where(live, lf, 0.0)
        ig = jnp.where(live, ig, -jnp.inf)
    r = _iota((chunk, chunk), 0)
    c = _iota((chunk, chunk), 1)
    tril = c <= r
    lf_hi = lf.astype(BF16)
    bc2 = jnp.dot(jnp.where(tril, 1.0, 0.0).astype(BF16),
                  jnp.concatenate([lf_hi, (lf - lf_hi.astype(F32)).astype(BF16)], axis=1), preferred_element_type=F32)
    bc = bc2[:, :PAIR_W] + bc2[:, PAIR_W:]
    a = ig - bc
    a_t = a.T
    q_all = q_ref[...]
    k_all = k_ref[...] * (ML_DIM ** -0.5)
    v_all = v_ref[...]
    rows_st = ML_HEADS * chunk
    q_st = jnp.where(_iota((rows_st, HEAD_W), 0) // chunk == _iota((rows_st, HEAD_W), 1) // ML_DIM,
                     jnp.concatenate([q_all] * ML_HEADS, axis=0), 0.0).astype(BF16)
    qk_st = _dot_nt(q_st, k_all)
    qc_st = _dot(q_st, c_sc[...].reshape(ML_HEADS * ML_DIM, ML_DIM))
    wts, ms, inters, wks, decays = [], [], [], [], []
    for h in range(ML_HEADS):
        sl = slice(h * ML_DIM, (h + 1) * ML_DIM)
        bc_h = bc[:, h:h + 1]
        a_row = a_t[h:h + 1, :]
        m0 = m_sc[h:h + 1, 0:1]
        cm = jnp.max(jnp.where(tril, a_row, -jnp.inf), axis=-1, keepdims=True)
        m_h = jnp.maximum(bc_h + m0, cm + bc_h)
        dmat = jnp.exp(jnp.where(tril, bc_h - m_h + a_row, -jnp.inf))
        wts.append(dmat * qk_st[h * chunk:(h + 1) * chunk])
        ms.append(m_h)
        inters.append(jnp.exp(bc_h + m0 - m_h))
        bc_l = bc_h[chunk - 1:chunk, :]
        m_l = m_h[chunk - 1:chunk, :]
        decays.append(jnp.exp(bc_l + m0 - m_l))
        wks.append(jnp.exp(bc_l + a[:, h:h + 1] - m_l) * k_all[:, sl])
        m_sc[h:h + 1, :] = jnp.broadcast_to(m_l, (1, PAIR_W))
    wv_st = _dot(jnp.concatenate(wts, axis=0), v_all)
    wk_all = jnp.concatenate(wks, axis=1)
    upd = _dot_tn(wk_all, v_all)
    for h in range(ML_HEADS):
        sl = slice(h * ML_DIM, (h + 1) * ML_DIM)
        rs = slice(h * chunk, (h + 1) * chunk)
        n_h = n_sc[h:h + 1, :]
        num = inters[h] * qc_st[rs] + wv_st[rs, sl]
        den = (inters[h] * jnp.sum(q_all[:, sl] * n_h, axis=-1, keepdims=True)
               + jnp.sum(wts[h], axis=-1, keepdims=True))
        hv = num / jnp.maximum(jnp.abs(den), jnp.exp(-ms[h]))
        hn = hv * lax.rsqrt(jnp.mean(hv * hv, axis=-1, keepdims=True) + EPS) * gn_ref[:, sl]
        out_ref[:, sl] = hn * _sigmoid(og_ref[:, sl]) * _silu(z_ref[:, sl])
        c_sc[h] = decays[h] * c_sc[h] + upd[sl, sl]
        n_sc[h:h + 1, :] = decays[h] * n_h + jnp.sum(wks[h], axis=0, keepdims=True)

    @pl.when(ci == pl.num_programs(1) - 1)
    def _():
        c_out[...] = c_sc[...]
        n_out[...] = n_sc[...]
        m_out[...] = m_sc[...]


def _mlstm(q, k, v, og, z, gates, bias, gn, c0, n0, m0, nb, chunk, valid):
    m = q.shape[0]
    n_chunks = m // (nb * chunk)
    tok = pl.BlockSpec((chunk, HEAD_W), lambda b, ci: (b * n_chunks + ci, 0))
    st_c = pl.BlockSpec((None, ML_HEADS, ML_DIM, ML_DIM), lambda b, ci: (b, 0, 0, 0))
    st_v = pl.BlockSpec((None, SAMPLE_PAD, ML_DIM), lambda b, ci: (b, 0, 0))
    return pl.pallas_call(
        functools.partial(_mlstm_kernel, chunk=chunk, valid=valid),
        grid=(nb, n_chunks),
        in_specs=[tok, tok, tok, tok, tok,
                  pl.BlockSpec((chunk, PAIR_W), lambda b, ci: (b * n_chunks + ci, 0)),
                  pl.BlockSpec((1, PAIR_W), lambda b, ci: (0, 0)),
                  pl.BlockSpec((1, HEAD_W), lambda b, ci: (0, 0)),
                  st_c, st_v, st_v],
        out_specs=[tok, st_c, st_v, st_v],
        out_shape=[jax.ShapeDtypeStruct((m, HEAD_W), F32),
                   jax.ShapeDtypeStruct((nb, ML_HEADS, ML_DIM, ML_DIM), F32),
                   jax.ShapeDtypeStruct((nb, SAMPLE_PAD, ML_DIM), F32),
                   jax.ShapeDtypeStruct((nb, SAMPLE_PAD, ML_DIM), F32)],
        scratch_shapes=[pltpu.VMEM((ML_HEADS, ML_DIM, ML_DIM), F32),
                        pltpu.VMEM((SAMPLE_PAD, ML_DIM), F32),
                        pltpu.VMEM((SAMPLE_PAD, ML_DIM), F32)],
        name="mlstm",
        compiler_params=_params(("arbitrary", "arbitrary")),
    )(q, k, v, og, z, gates, bias, gn, c0, n0, m0)


def _gla_kernel(q_ref, k_ref, v_ref, z_ref, gg_ref, wg_ref, bg_ref, gn_ref, s0_ref, out_ref, s_out,
                st_sc, kb, gb, vb, qa_sc, kc_sc, dec_sc, oi_sc, *, tb, sub, valid):
    ti = pl.program_id(1)
    kw = GLA_HEADS * GLA_DK

    @pl.when(ti == 0)
    def _():
        for p in range(2):
            st_sc[p] = s0_ref[p].T
        kb[0:sub, :] = jnp.zeros((sub, kw), F32)
        gb[0:sub, :] = jnp.zeros((sub, kw), F32)
        vb[0:sub, :] = jnp.zeros((sub, HEAD_W), F32)

    lg = _log_sigmoid(_dot(gg_ref[...], wg_ref[...]) + bg_ref[...]) * (1.0 / GLA_TAU)
    k = k_ref[...]
    if valid < tb:
        live = _iota((tb, kw), 0) < valid
        lg = jnp.where(live, lg, 0.0)
        k = jnp.where(live, k, 0.0)
    r = _iota((tb, tb), 0)
    c = _iota((tb, tb), 1)
    same = (r // sub) == (c // sub)
    sums = jnp.concatenate([jnp.where(same & (c <= r), 1.0, 0.0), jnp.where(same, 1.0, 0.0)], axis=0).astype(BF16)
    lg_hi = lg.astype(BF16)
    parts = jnp.dot(sums, jnp.concatenate([lg_hi, (lg - lg_hi.astype(F32)).astype(BF16)], axis=1),
                    preferred_element_type=F32)
    g = parts[:tb, :kw] + parts[:tb, kw:]
    gend = parts[tb:, :kw] + parts[tb:, kw:]
    q = q_ref[...] * (GLA_DK ** -0.5)
    v = v_ref[...]
    qa_sc[...] = q * jnp.exp(g)
    kc_sc[...] = k * jnp.exp(gend - g)
    dec_sc[...] = jnp.exp(gend)
    kb[sub:sub + tb, :] = k
    gb[sub:sub + tb, :] = g
    vb[sub:sub + tb, :] = v

    head_spread = jnp.where(_iota((kw, HEAD_W), 0) // GLA_DK == _iota((kw, HEAD_W), 1) // GLA_DV, 1.0, 0.0).astype(BF16)
    rmod = _iota((tb, 1), 0) % sub
    oi = jnp.zeros((tb, HEAD_W), F32)
    for d in range(sub):
        lo = sub - d
        dec = jnp.exp(jnp.where(rmod >= d, g - gb[lo:lo + tb, :], NEG))
        oi = oi + _dot(q * kb[lo:lo + tb, :] * dec, head_spread) * vb[lo:lo + tb, :]
    oi_sc[...] = oi

    lane_head = _iota((2 * sub, PAIR_W), 1) // GLA_DK
    row_head = _iota((2 * sub, PAIR_W), 0) // sub
    for i in range(tb // sub):
        rows = slice(i * sub, (i + 1) * sub)
        qa = qa_sc[rows, :]
        kc = kc_sc[rows, :]
        dec = dec_sc[rows, :][0:1, :]
        for p in range(2):
            ps = slice(p * PAIR_W, (p + 1) * PAIR_W)
            vs = slice(2 * p * GLA_DV, (2 * p + 2) * GLA_DV)
            st = st_sc[p]
            own = lane_head == row_head
            o2 = _dot_nt(jnp.where(own, jnp.concatenate([qa[:, ps]] * 2, axis=0), 0.0), st)
            v2 = jnp.concatenate([v_ref[rows, vs][:, :GLA_DV], v_ref[rows, vs][:, GLA_DV:]], axis=0)
            st_sc[p] = st * dec[:, ps] + _dot_tn(v2, jnp.where(own, jnp.concatenate([kc[:, ps]] * 2, axis=0), 0.0))
            for hh in range(2):
                hs = slice((2 * p + hh) * GLA_DV, (2 * p + hh + 1) * GLA_DV)
                o = o2[hh * sub:(hh + 1) * sub] + oi_sc[rows, hs]
                on = o * lax.rsqrt(jnp.mean(o * o, axis=-1, keepdims=True) + EPS) * gn_ref[:, hs]
                out_ref[rows, hs] = on * _silu(z_ref[rows, hs])

    @pl.when(ti == pl.num_programs(1) - 1)
    def _():
        for p in range(2):
            s_out[p] = st_sc[p].T


def _gla(q, k, v, z, gg, wg, bg, gn, s0, nb, tb, sub, valid):
    m = q.shape[0]
    nt = m // (nb * tb)
    kw = GLA_HEADS * GLA_DK
    tok_k = pl.BlockSpec((tb, kw), lambda b, t: (b * nt + t, 0))
    tok_v = pl.BlockSpec((tb, HEAD_W), lambda b, t: (b * nt + t, 0))
    st = pl.BlockSpec((None, 2, PAIR_W, GLA_DV), lambda b, t: (b, 0, 0, 0))
    return pl.pallas_call(
        functools.partial(_gla_kernel, tb=tb, sub=sub, valid=valid),
        grid=(nb, nt),
        in_specs=[tok_k, tok_k, tok_v, tok_v,
                  pl.BlockSpec((tb, PAIR_W), lambda b, t: (b * nt + t, 0)),
                  pl.BlockSpec((PAIR_W, kw), lambda b, t: (0, 0)),
                  pl.BlockSpec((1, kw), lambda b, t: (0, 0)),
                  pl.BlockSpec((1, HEAD_W), lambda b, t: (0, 0)),
                  st],
        out_specs=[tok_v, st],
        out_shape=[jax.ShapeDtypeStruct((m, HEAD_W), F32),
                   jax.ShapeDtypeStruct((nb, 2, PAIR_W, GLA_DV), F32)],
        scratch_shapes=[pltpu.VMEM((2, GLA_DV, PAIR_W), F32),
                        pltpu.VMEM((sub + tb, kw), F32),
                        pltpu.VMEM((sub + tb, kw), F32),
                        pltpu.VMEM((sub + tb, HEAD_W), F32),
                        pltpu.VMEM((tb, kw), F32),
                        pltpu.VMEM((tb, kw), F32),
                        pltpu.VMEM((tb, kw), F32),
                        pltpu.VMEM((tb, HEAD_W), F32)],
        name="gla",
        compiler_params=_params(("arbitrary", "arbitrary")),
    )(q, k, v, z, gg, wg, bg, gn, s0)


def _top3_rows(s, n_valid):
    nb = s.shape[0]
    nrow = _iota(s.shape, 0)
    live = nrow < n_valid
    s = jnp.where(live, s, -jnp.inf)
    chosen = jnp.zeros(s.shape, F32)
    picks = []
    for _ in range(MB_TOPK):
        mx = jnp.max(s, axis=0, keepdims=True)
        idx = jnp.min(jnp.where(s == mx, nrow, nb), axis=0, keepdims=True)
        pick = nrow == idx
        chosen = jnp.where(pick, jnp.where(live, 1.0, 0.0), chosen)
        s = jnp.where(pick, -jnp.inf, s)
        picks.append(idx)
    return picks, chosen


KEY_CHUNK = 64


def _moba_prompt_kernel(q_ref, kt_ref, vt_ref, z_ref, o_ref,
                        ktok_sc, vtbd_sc, km_sc, bias_sc, sel2_sc, sel_sc, qtm_sc, s_sc, cm_sc, p_sc, acc_sc, m_sc, l_sc,
                        *, tq, nblk):
    p = pl.program_id(1)
    i = pl.program_id(2)
    slope2 = [jnp.exp2(-(jnp.full((1, tq), 2 * p + hh + 1, jnp.int32)).astype(F32)) * LOG2E for hh in range(2)]

    row_head = _iota((PAIR_W, tq), 0) // MB_DIM

    @pl.when(i == 0)
    def _():
        lane = _iota((PAIR_W, PAIR_W), 1)
        feat_head = _iota((PAIR_W, MB_BLOCK), 0) // MB_DIM
        km = jnp.zeros((PAIR_W, PAIR_W), F32)
        for n in range(nblk):
            cols = slice(n * MB_BLOCK, (n + 1) * MB_BLOCK)
            blk = kt_ref[:, cols]
            ktok_sc[cols, :] = blk.T.astype(BF16)
            km = jnp.where(lane == n, jnp.mean(blk, axis=1, keepdims=True), km)
            vblk = vt_ref[:, cols]
            for hh in range(2):
                vtbd_sc[:, (2 * n + hh) * MB_BLOCK:(2 * n + hh + 1) * MB_BLOCK] = (
                    jnp.where(feat_head == hh, vblk, 0.0).astype(BF16))
        km_t = km.T
        blk_head = _iota((PAIR_W, PAIR_W), 1) // MB_DIM
        for hh in range(2):
            km_sc[hh * nblk:(hh + 1) * nblk, :] = jnp.where(blk_head == hh, km_t, 0.0)[0:nblk]
        key_off = _iota((MB_BLOCK, tq), 0).astype(F32)
        for hh in range(2):
            bias_sc[hh] = slope2[hh] * key_off

    q_t = q_ref[...].T
    s_sel = jnp.dot(km_sc[...], q_t, precision=lax.Precision.HIGHEST, preferred_element_type=F32)
    for hh in range(2):
        _, chosen = _top3_rows(s_sel[hh * nblk:(hh + 1) * nblk], i)
        sel2_sc[...] = chosen
        for n in range(nblk):
            sel_sc[hh, n] = sel2_sc[n:n + 1, :]
        qtm_sc[:, hh * tq:(hh + 1) * tq] = jnp.where(row_head == hh, q_t * (MB_DIM ** -0.5 * LOG2E), 0.0).astype(BF16)

    n_chunks = MB_BLOCK // KEY_CHUNK

    def scores(n, slot, diag):
        start = pl.multiple_of(n * MB_BLOCK, MB_BLOCK)
        s = jnp.dot(ktok_sc[pl.ds(start, MB_BLOCK), :], qtm_sc[...], preferred_element_type=F32)
        for hh in range(2):
            cm = None
            for kc in range(n_chunks):
                rows = slice(kc * KEY_CHUNK, (kc + 1) * KEY_CHUNK)
                sc = s[rows, hh * tq:(hh + 1) * tq] + bias_sc[hh, rows]
                if diag:
                    key = _iota((KEY_CHUNK, tq), 0) + kc * KEY_CHUNK
                    sc = jnp.where(key <= _iota((KEY_CHUNK, tq), 1), sc, NEG)
                s_sc[slot, hh, rows] = sc
                c = jnp.max(sc, axis=0, keepdims=True)
                cm = c if cm is None else jnp.maximum(cm, c)
            cm_sc[slot, hh] = cm

    def attend(n, slot, diag):
        alphas = []
        for hh in range(2):
            cm = cm_sc[slot, hh]
            if diag:
                m_new = cm
                shift = cm
            else:
                c = slope2[hh] * ((n - i) * MB_BLOCK).astype(F32)
                ok = sel_sc[hh, n] > 0.5
                m_old = m_sc[hh]
                m_new = jnp.maximum(m_old, jnp.where(ok, cm + c, NEG))
                shift = jnp.where(ok, m_new - c, -NEG)
            lsum = jnp.zeros((1, tq), F32)
            for kc in range(n_chunks):
                rows = slice(kc * KEY_CHUNK, (kc + 1) * KEY_CHUNK)
                pr = jnp.exp2(s_sc[slot, hh, rows] - shift)
                lsum = lsum + jnp.sum(pr, axis=0, keepdims=True)
                p_sc[hh * MB_BLOCK + kc * KEY_CHUNK:hh * MB_BLOCK + (kc + 1) * KEY_CHUNK, :] = pr.astype(BF16)
            if diag:
                l_sc[hh] = lsum
            else:
                alpha = jnp.exp2(m_old - m_new)
                l_sc[hh] = alpha * l_sc[hh] + lsum
                alphas.append(alpha)
            m_sc[hh] = m_new
        vt = vtbd_sc[:, pl.ds(pl.multiple_of(n * (2 * MB_BLOCK), 2 * MB_BLOCK), 2 * MB_BLOCK)]
        pv = jnp.dot(vt, p_sc[...], preferred_element_type=F32)
        if diag:
            acc_sc[...] = pv
        else:
            acc_sc[...] = acc_sc[...] * jnp.where(row_head == 0, alphas[0], alphas[1]) + pv

    scores(i, 0, True)

    @pl.when(i > 0)
    def _():
        scores(0, 1, False)

    attend(i, 0, True)

    def body(n, _):
        attend(n, (n + 1) % 2, False)
        scores(jnp.minimum(n + 1, i - 1), n % 2, False)
        return 0

    lax.fori_loop(0, i, body, 0)
    out_t = acc_sc[...] * jnp.where(row_head == 0, 1.0 / l_sc[0], 1.0 / l_sc[1])
    o_ref[...] = out_t.T * _silu(z_ref[...])


def _moba_prompt(q, kt, vt, z):
    b, s, _ = q.shape
    tq = MB_BLOCK
    nblk = s // MB_BLOCK
    assert nblk <= PAIR_W
    blk_q = pl.BlockSpec((None, tq, PAIR_W), lambda bi, p, i: (bi, i, p))
    blk_kv = pl.BlockSpec((None, PAIR_W, s), lambda bi, p, i: (bi, p, 0))
    return pl.pallas_call(
        functools.partial(_moba_prompt_kernel, tq=tq, nblk=nblk),
        grid=(b, HEAD_W // PAIR_W, s // tq),
        in_specs=[blk_q, blk_kv, blk_kv, blk_q],
        out_specs=blk_q,
        out_shape=jax.ShapeDtypeStruct((b, s, HEAD_W), F32),
        scratch_shapes=[pltpu.VMEM((s, PAIR_W), BF16),
                        pltpu.VMEM((PAIR_W, 2 * s), BF16),
                        pltpu.VMEM((2 * nblk, PAIR_W), F32),
                        pltpu.VMEM((2, MB_BLOCK, tq), F32),
                        pltpu.VMEM((nblk, tq), F32),
                        pltpu.VMEM((2, nblk, 1, tq), F32),
                        pltpu.VMEM((PAIR_W, 2 * tq), BF16),
                        pltpu.VMEM((2, 2, MB_BLOCK, tq), F32),
                        pltpu.VMEM((2, 2, 1, tq), F32),
                        pltpu.VMEM((2 * MB_BLOCK, tq), BF16),
                        pltpu.VMEM((PAIR_W, tq), F32),
                        pltpu.VMEM((2, 1, tq), F32),
                        pltpu.VMEM((2, 1, tq), F32)],
        name="moba_prompt",
        compiler_params=_params(("arbitrary", "arbitrary", "arbitrary")),
    )(q, kt, vt, z)


_MEAN_PAGES = 32


def _top3_lanes(s, n_valid):
    width = s.shape[1]
    lane = _iota(s.shape, 1)
    s = jnp.where(lane < n_valid, s, -jnp.inf)
    picks = []
    for _ in range(MB_TOPK):
        mx = jnp.max(s, axis=1, keepdims=True)
        idx = jnp.min(jnp.where(s == mx, lane, width), axis=1, keepdims=True)
        s = jnp.where(lane == idx, -jnp.inf, s)
        picks.append(idx)
    return picks


def _mb_select_kernel(pt_ref, *refs, n_blocks):
    pages = refs[:_MEAN_PAGES]
    q_ref, idx_ref, km_sc = refs[_MEAN_PAGES:]
    s = pl.program_id(1)
    per_blk = MB_BLOCK // PAGE
    blk_per_step = _MEAN_PAGES // per_blk

    @pl.when(s == 0)
    def _():
        km_sc[...] = jnp.zeros_like(km_sc)

    lane = _iota((HEAD_W, PAIR_W), 1)
    km = km_sc[...]
    for j in range(blk_per_step):
        both = sum(pages[per_blk * j + e][...] for e in range(per_blk))
        km = jnp.where(lane == s * blk_per_step + j, jnp.sum(both, axis=1, keepdims=True) * (1.0 / MB_BLOCK), km)
    km_sc[...] = km

    @pl.when(s == pl.num_programs(1) - 1)
    def _():
        qbd = _block_diag_queries(q_ref[...], PAIR_W, MB_DIM)
        sc = jnp.dot(qbd, km, precision=lax.Precision.HIGHEST, preferred_element_type=F32)
        picks = _top3_lanes(sc, n_blocks)
        lane_o = _iota((PAIR_W, PAIR_W), 1)
        out = jnp.zeros((PAIR_W, PAIR_W), jnp.int32)
        for k, pk in enumerate(picks):
            out = jnp.where(lane_o == k, pk, out)
        idx_ref[...] = out


def _mb_select(q, cache_k, page_table):
    nb = q.shape[0]
    n_pages = page_table.shape[1]
    steps = n_pages // _MEAN_PAGES
    n_blocks = n_pages * PAGE // MB_BLOCK
    assert n_pages % _MEAN_PAGES == 0 and MB_TOPK <= n_blocks <= PAIR_W
    page_specs = [pl.BlockSpec((None, HEAD_W, PAGE), lambda bi, s, pt, j=j: (pt[bi * n_pages + s * _MEAN_PAGES + j], 0, 0))
                  for j in range(_MEAN_PAGES)]
    return pl.pallas_call(
        functools.partial(_mb_select_kernel, n_blocks=n_blocks),
        grid_spec=pltpu.PrefetchScalarGridSpec(
            num_scalar_prefetch=1,
            grid=(nb, steps),
            in_specs=page_specs + [pl.BlockSpec((None, SAMPLE_PAD, HEAD_W), lambda bi, s, pt: (bi, 0, 0))],
            out_specs=pl.BlockSpec((None, PAIR_W, PAIR_W), lambda bi, s, pt: (bi, 0, 0)),
            scratch_shapes=[pltpu.VMEM((HEAD_W, PAIR_W), F32)],
        ),
        out_shape=jax.ShapeDtypeStruct((nb, PAIR_W, PAIR_W), jnp.int32),
        name="moba_select",
        compiler_params=_params(("arbitrary", "arbitrary")),
    )(page_table.reshape(-1), *([cache_k] * _MEAN_PAGES), q)


def _mb_sample_kernel(pt_ref, sel_ref, q_ref, kn_ref, vn_ref, z_ref, idx_ref, ck_hbm, cv_hbm, o_ref,
                      kbuf, vbuf, sem, kpad, vpad, s_sc, *, n_pages, t_real):
    b = pl.program_id(0)
    per_blk = MB_BLOCK // PAGE
    n_grp = MB_TOPK * t_real
    n_sel = n_grp * MB_BLOCK
    past = n_pages * PAGE

    def copies(h, k, t, j):
        blk = sel_ref[(b * MB_HEADS * t_real + h * t_real + t) * MB_TOPK + k]
        phys = pt_ref[b * n_pages + blk * per_blk + j]
        hs = pl.ds(h * MB_DIM, MB_DIM)
        ls = pl.ds(((k * t_real + t) * per_blk + j) * PAGE, PAGE)
        return (pltpu.make_async_copy(ck_hbm.at[phys, hs, :], kbuf.at[hs, ls], sem.at[0]),
                pltpu.make_async_copy(cv_hbm.at[phys, hs, :], vbuf.at[hs, ls], sem.at[1]))

    todo = [(h, k, t, j) for h in range(MB_HEADS) for k in range(MB_TOPK) for t in range(t_real)
            for j in range(per_blk)]
    for a in todo:
        for cp in copies(*a):
            cp.start()

    @pl.when(b == 0)
    def _():
        kpad[...] = jnp.zeros_like(kpad)
        vpad[...] = jnp.zeros_like(vpad)

    kpad[0:SAMPLE_PAD, :] = kn_ref[...]
    vpad[0:SAMPLE_PAD, :] = vn_ref[...]

    nq = PAIR_W
    qbd = _block_diag_queries(q_ref[...] * (MB_DIM ** -0.5), nq, MB_DIM).astype(BF16)
    row = _iota((nq, 1), 0)
    t_row = row % SAMPLE_PAD
    slope = jnp.exp2(-(row // SAMPLE_PAD + 1).astype(F32))

    key = _iota((nq, PAGE), 1)
    s_new = _dot_nt(qbd, kpad[...]) - slope * (t_row - key).astype(F32)
    s_new = jnp.where((key <= t_row) & (key < t_real), s_new, NEG)
    s_sc[:, n_sel:n_sel + PAGE] = s_new
    m = jnp.max(s_new, axis=1, keepdims=True)

    for a in todo:
        for cp in copies(*a):
            cp.wait()

    key_in_blk = _iota((nq, MB_BLOCK), 1)
    for g in range(n_grp):
        k, t = divmod(g, t_real)
        lanes = slice(g * MB_BLOCK, (g + 1) * MB_BLOCK)
        pos = idx_ref[:, k:k + 1] * MB_BLOCK + key_in_blk
        sg = _dot(qbd, kbuf[:, lanes]) - slope * (past + t_row - pos).astype(F32)
        sg = jnp.where(t_row == t, sg, NEG)
        s_sc[:, lanes] = sg
        m = jnp.maximum(m, jnp.max(sg, axis=1, keepdims=True))

    pr = jnp.exp(s_sc[...] - m)
    num = _dot_nt(pr[:, :n_sel], vbuf[...]) + _dot(pr[:, n_sel:], vpad[...])
    res = num / jnp.sum(pr, axis=1, keepdims=True)
    o_ref[...] = _gather_heads(res, MB_DIM) * _silu(z_ref[...])


def _mb_sample(q, kn, vn, z, idx, cache_k, cache_v, page_table, t_real):
    nb = q.shape[0]
    n_pages = page_table.shape[1]
    n_sel = MB_TOPK * t_real * MB_BLOCK
    sel = idx[:, :MB_HEADS * SAMPLE_PAD, :MB_TOPK].reshape(nb, MB_HEADS, SAMPLE_PAD, MB_TOPK)[:, :, :t_real]
    blk = pl.BlockSpec((None, SAMPLE_PAD, HEAD_W), lambda bi, pt, sl: (bi, 0, 0))
    hbm = pl.BlockSpec(memory_space=pl.ANY)
    return pl.pallas_call(
        functools.partial(_mb_sample_kernel, n_pages=n_pages, t_real=t_real),
        grid_spec=pltpu.PrefetchScalarGridSpec(
            num_scalar_prefetch=2,
            grid=(nb,),
            in_specs=[blk, blk, blk, blk,
                      pl.BlockSpec((None, PAIR_W, PAIR_W), lambda bi, pt, sl: (bi, 0, 0)),
                      hbm, hbm],
            out_specs=blk,
            scratch_shapes=[pltpu.VMEM((HEAD_W, n_sel), F32),
                            pltpu.VMEM((HEAD_W, n_sel), F32),
                            pltpu.SemaphoreType.DMA((2,)),
                            pltpu.VMEM((PAGE, HEAD_W), F32),
                            pltpu.VMEM((PAGE, HEAD_W), F32),
                            pltpu.VMEM((PAIR_W, n_sel + PAGE), F32)],
        ),
        out_shape=jax.ShapeDtypeStruct((nb, SAMPLE_PAD, HEAD_W), F32),
        name="moba_sample",
        compiler_params=_params(("arbitrary",), 48),
    )(page_table.reshape(-1), sel.reshape(-1), q, kn, vn, z, idx, cache_k, cache_v)


def _pad_cols(w, width):
    return jnp.pad(w, ((0, 0), (0, width - w.shape[1])))


def _pages_feature_major(cache):
    n_pool, page, heads, dim = cache.shape
    return jnp.transpose(cache, (0, 2, 3, 1)).reshape(n_pool, heads * dim, page)


def kernel(x_prompt, x_sample, cache_sb_k, cache_sb_v, cache_mb_k, cache_mb_v, state_ml_C, state_ml_n, state_ml_m, state_gla_S, page_table, c_prompt, c_sample, w_mod, b_mod, norm_g, final_g, w_in_even, w_out_even, ml_b_i, ml_b_f, ml_gn, w_in_odd, w_out_odd, gla_w_g2, gla_b_g, gla_gn):
    bsz, seq, _ = x_prompt.shape
    db, t_real, _ = x_sample.shape
    mp = bsz * seq
    ms = db * SAMPLE_PAD

    rows = -(-(bsz + db) // 8) * 8
    c_all = jnp.pad(jnp.concatenate([c_prompt, c_sample], axis=0), ((0, rows - bsz - db), (0, 0)))
    mod = _modulation(c_all, w_mod, b_mod)
    mod_p = mod[:, :bsz, None, :]
    mod_s = jnp.repeat(mod[:, bsz:bsz + db], SAMPLE_PAD, axis=1)[:, None]

    xp = x_prompt.reshape(mp, D_MODEL)
    xs = jnp.pad(x_sample, ((0, 0), (0, SAMPLE_PAD - t_real), (0, 0))).reshape(ms, D_MODEL)
    tm = 256

    w_even = jnp.concatenate([w_in_even[0, :, :9 * HEAD_W], _pad_cols(w_in_even[0, :, 9 * HEAD_W:], PAIR_W)],
                             axis=1).astype(BF16)
    widths_even = (HEAD_W,) * 9 + (PAIR_W,)
    gate_bias = _pad_cols(jnp.concatenate([ml_b_i[0], ml_b_f[0]])[None, :], PAIR_W)
    ml_gn0 = ml_gn[0].reshape(1, HEAD_W)
    w_out0 = w_out_even[0].astype(BF16)

    w_even_tok = jnp.concatenate([w_even[:, :HEAD_W], w_even[:, 3 * HEAD_W:]], axis=1)
    w_even_kv = w_even[:, HEAD_W:3 * HEAD_W].T
    sq, sz, mq, mk, mv, mo, mz, mg, sk, sv = _inproj(xp, mod_p[0], norm_g[0], w_even_tok, (HEAD_W,) * 7 + (PAIR_W,),
                                                     seq, tm, w_even_kv, (HEAD_W, HEAD_W))
    rp = lambda u: u.reshape(bsz, seq, HEAD_W)
    a_p = _sb_prompt(rp(sq), sk, sv, rp(sz)).reshape(mp, HEAD_W)
    zc = jnp.zeros((bsz, ML_HEADS, ML_DIM, ML_DIM), F32)
    zv = jnp.zeros((bsz, SAMPLE_PAD, ML_DIM), F32)
    b_p, mlc_p, mln_p, mlm_p = _mlstm(mq, mk, mv, mo, mz, mg, gate_bias, ml_gn0, zc, zv, zv, bsz, ML_CHUNK, ML_CHUNK)
    x1p = _outproj(a_p, b_p, xp, mod_p[0], w_out0, final_g, seq, tm, False)
    sbk_p, sbv_p = sk, sv

    sq, sk, sv, sz, mq, mk, mv, mo, mz, mg = _inproj(xs, mod_s[0], norm_g[0], w_even, widths_even, ms, ms)
    r3 = lambda u: u.reshape(db, SAMPLE_PAD, HEAD_W)
    a_s = _sb_sample(r3(sq), r3(sk), r3(sv), r3(sz), _pages_feature_major(cache_sb_k[0]),
                     _pages_feature_major(cache_sb_v[0]), page_table, t_real).reshape(ms, HEAD_W)
    n0 = jnp.pad(state_ml_n[0], ((0, 0), (0, SAMPLE_PAD - ML_HEADS), (0, 0)))
    m0 = jnp.broadcast_to(jnp.pad(state_ml_m[0], ((0, 0), (0, SAMPLE_PAD - ML_HEADS)))[:, :, None],
                          (db, SAMPLE_PAD, ML_DIM))
    b_s, mlc_s, mln_s, mlm_s = _mlstm(mq, mk, mv, mo, mz, mg, gate_bias, ml_gn0, state_ml_C[0], n0, m0,
                                      db, SAMPLE_PAD, t_real)
    x1s = _outproj(a_s, b_s, xs, mod_s[0], w_out0, final_g, ms, ms, False)
    sbk_s, sbv_s = r3(sk)[:, :t_real], r3(sv)[:, :t_real]

    kw = GLA_HEADS * GLA_DK
    g0 = 2 * kw + 2 * HEAD_W
    w_odd = jnp.concatenate([w_in_odd[0, :, :g0], w_in_odd[0, :, g0 + GLA_RANK:],
                             _pad_cols(w_in_odd[0, :, g0:g0 + GLA_RANK], PAIR_W)], axis=1).astype(BF16)
    widths_odd = (kw, kw) + (HEAD_W,) * 6 + (PAIR_W,)
    wg = jnp.pad(gla_w_g2[0], ((0, PAIR_W - GLA_RANK), (0, 0))).astype(BF16)
    bg = gla_b_g[0].reshape(1, kw)
    gla_gn0 = gla_gn[0].reshape(1, HEAD_W)
    w_out1 = w_out_odd[0].astype(BF16)

    c0 = 2 * kw + 3 * HEAD_W
    w_odd_tok = jnp.concatenate([w_odd[:, :c0], w_odd[:, c0 + 2 * HEAD_W:]], axis=1)
    w_odd_kv = w_odd[:, c0:c0 + 2 * HEAD_W].T
    gq, gk, gv, gz, dq, dz, gg, dk, dv = _inproj(x1p, mod_p[1], norm_g[1], w_odd_tok,
                                                 (kw, kw) + (HEAD_W,) * 4 + (PAIR_W,), seq, tm,
                                                 w_odd_kv, (HEAD_W, HEAD_W))
    s_zero = jnp.zeros((bsz, 2, PAIR_W, GLA_DV), F32)
    c_p, glas_p = _gla(gq, gk, gv, gz, gg, wg, bg, gla_gn0, s_zero, bsz, 256, GLA_SUB, 256)
    d_p = _moba_prompt(rp(dq), dk, dv, rp(dz)).reshape(mp, HEAD_W)
    y_p = _outproj(c_p, d_p, x1p, mod_p[1], w_out1, final_g, seq, tm, True)
    mbk_p, mbv_p = dk, dv

    gq, gk, gv, gz, dq, dk, dv, dz, gg = _inproj(x1s, mod_s[1], norm_g[1], w_odd, widths_odd, ms, ms)
    c_s, glas_s = _gla(gq, gk, gv, gz, gg, wg, bg, gla_gn0, state_gla_S[0].reshape(db, 2, PAIR_W, GLA_DV),
                       db, SAMPLE_PAD, SAMPLE_PAD, t_real)
    ck = _pages_feature_major(cache_mb_k[0])
    cv = _pages_feature_major(cache_mb_v[0])
    idx = _mb_select(r3(dq), ck, page_table)
    d_s = _mb_sample(r3(dq), r3(dk), r3(dv), r3(dz), idx, ck, cv, page_table, t_real).reshape(ms, HEAD_W)
    y_s = _outproj(c_s, d_s, x1s, mod_s[1], w_out1, final_g, ms, ms, True)
    mbk_s, mbv_s = r3(dk)[:, :t_real], r3(dv)[:, :t_real]

    kv_p = lambda u, h, d: jnp.transpose(u.reshape(1, bsz, h, d, seq), (0, 1, 4, 2, 3))
    kv_s = lambda u, h, d: u.reshape(1, db, t_real, h, d)
    return (y_p.reshape(bsz, seq, D_MODEL), y_s.reshape(db, SAMPLE_PAD, D_MODEL)[:, :t_real],
            kv_p(sbk_p, SB_HEADS, SB_DIM), kv_p(sbv_p, SB_HEADS, SB_DIM),
            kv_s(sbk_s, SB_HEADS, SB_DIM), kv_s(sbv_s, SB_HEADS, SB_DIM),
            mlc_p[None], mln_p[None, :, :ML_HEADS], mlm_p[None, :, :ML_HEADS, 0],
            mlc_s[None], mln_s[None, :, :ML_HEADS], mlm_s[None, :, :ML_HEADS, 0],
            glas_p.reshape(1, bsz, GLA_HEADS, GLA_DK, GLA_DV), glas_s.reshape(1, db, GLA_HEADS, GLA_DK, GLA_DV),
            kv_p(mbk_p, MB_HEADS, MB_DIM), kv_p(mbv_p, MB_HEADS, MB_DIM),
            kv_s(mbk_s, MB_HEADS, MB_DIM), kv_s(mbv_s, MB_HEADS, MB_DIM))
```

```python
import functools

import jax
import jax.numpy as jnp
from jax import lax
from jax.experimental import pallas as pl
from jax.experimental.pallas import tpu as pltpu

F32 = jnp.float32
BF16 = jnp.bfloat16

D_MODEL = 1024
EPS = 1e-6
PAGE = 128
SB_HEADS, SB_DIM = 8, 64
ML_HEADS, ML_DIM, ML_CHUNK = 4, 128, 64
GLA_HEADS, GLA_DK, GLA_DV, GLA_RANK, GLA_TAU = 4, 64, 128, 16, 16.0
MB_HEADS, MB_DIM, MB_BLOCK, MB_TOPK = 8, 64, 256, 3
HEAD_W = 512
PAIR_W = 128
SAMPLE_PAD = 8
NEG = -1e30
LOG2E = 1.4426950408889634
SB_SKIP = 88.0
SB_SKIP2 = SB_SKIP * 1.4426950408889634
GLA_SUB = 16


def _dot(a, b):
    return jnp.dot(a.astype(BF16), b.astype(BF16), preferred_element_type=F32)


def _dot_nt(a, b):
    return lax.dot_general(a.astype(BF16), b.astype(BF16), (((1,), (1,)), ((), ())), preferred_element_type=F32)


def _dot_tn(a, b):
    return lax.dot_general(a.astype(BF16), b.astype(BF16), (((0,), (0,)), ((), ())), preferred_element_type=F32)


def _softplus_tail(z):
    return jnp.log(1.0 + jnp.exp(-jnp.abs(z)))


def _log_sigmoid(z):
    return jnp.minimum(z, 0.0) - _softplus_tail(z)


def _sigmoid(z):
    return 1.0 / (1.0 + jnp.exp(-z))


def _silu(z):
    return z * _sigmoid(z)


def _iota(shape, axis):
    return lax.broadcasted_iota(jnp.int32, shape, axis)


def _params(sem, vmem_mb=None):
    kw = dict(dimension_semantics=sem)
    if vmem_mb is not None:
        kw["vmem_limit_bytes"] = vmem_mb * 1024 * 1024
    return pltpu.CompilerParams(**kw)


def _mod_kernel(c_ref, w_ref, b_ref, o_ref):
    o_ref[...] = _dot(_silu(c_ref[...]), w_ref[...]) + b_ref[...]


def _modulation(c_all, w_mod, b_mod):
    depth = w_mod.shape[0]
    rows = c_all.shape[0]
    return pl.pallas_call(
        _mod_kernel,
        grid=(depth, 3),
        in_specs=[
            pl.BlockSpec((rows, D_MODEL), lambda l, j: (0, 0)),
            pl.BlockSpec((None, D_MODEL, D_MODEL), lambda l, j: (l, 0, j)),
            pl.BlockSpec((None, 1, D_MODEL), lambda l, j: (l, 0, j)),
        ],
        out_specs=pl.BlockSpec((None, rows, D_MODEL), lambda l, j: (l, 0, j)),
        out_shape=jax.ShapeDtypeStruct((depth, rows, 3 * D_MODEL), F32),
        name="modulation",
        compiler_params=_params(("arbitrary", "arbitrary")),
    )(c_all, w_mod, b_mod.reshape(depth, 1, 3 * D_MODEL))


def _inproj_kernel(x_ref, mod_ref, g_ref, w_ref, *refs, widths, widths_t):
    x = x_ref[...]
    y = x * lax.rsqrt(jnp.mean(x * x, axis=-1, keepdims=True) + EPS) * g_ref[...]
    mod = mod_ref[...]
    h = y * (1.0 + mod[:, D_MODEL:2 * D_MODEL]) + mod[:, :D_MODEL]
    hb = h.astype(BF16)
    out_refs = refs[1:] if widths_t else refs
    off = 0
    for o_ref, wd in zip(out_refs, widths):
        o_ref[...] = jnp.dot(hb, w_ref[:, off:off + wd], preferred_element_type=F32)
        off += wd
    if widths_t:
        wt_ref = refs[0]
        ht = h.T.astype(BF16)
        off = 0
        for o_ref, wd in zip(out_refs[len(widths):], widths_t):
            o_ref[...] = jnp.dot(wt_ref[off:off + wd, :], ht, preferred_element_type=F32)
            off += wd


def _inproj(x, mod, g, w, widths, rows_per_group, tm, wt=None, widths_t=()):
    m = x.shape[0]
    r = mod.shape[1]
    n = w.shape[1]
    groups = m // rows_per_group
    per_group = rows_per_group // tm
    in_specs = [
        pl.BlockSpec((tm, D_MODEL), lambda i: (i, 0)),
        pl.BlockSpec((None, r, 3 * D_MODEL), lambda i: ((i * tm) // rows_per_group, 0, 0)),
        pl.BlockSpec((1, D_MODEL), lambda i: (0, 0)),
        pl.BlockSpec((D_MODEL, n), lambda i: (0, 0)),
    ]
    args = [x, mod, g.reshape(1, D_MODEL), w]
    if widths_t:
        in_specs.append(pl.BlockSpec(wt.shape, lambda i: (0, 0)))
        args.append(wt)
    return pl.pallas_call(
        functools.partial(_inproj_kernel, widths=widths, widths_t=widths_t),
        grid=(m // tm,),
        in_specs=in_specs,
        out_specs=([pl.BlockSpec((tm, wd), lambda i: (i, 0)) for wd in widths]
                   + [pl.BlockSpec((None, wd, tm), lambda i: (i // per_group, 0, i % per_group)) for wd in widths_t]),
        out_shape=([jax.ShapeDtypeStruct((m, wd), F32) for wd in widths]
                   + [jax.ShapeDtypeStruct((groups, wd, rows_per_group), F32) for wd in widths_t]),
        name="inproj",
        compiler_params=_params(("arbitrary",), 56),
    )(*args)


def _outproj_kernel(a_ref, b_ref, x_ref, mod_ref, w_ref, fg_ref, o_ref, *, final):
    y = _dot(a_ref[...], w_ref[:HEAD_W, :]) + _dot(b_ref[...], w_ref[HEAD_W:, :])
    xn = x_ref[...] + mod_ref[...][:, 2 * D_MODEL:] * y
    if final:
        xn = xn * lax.rsqrt(jnp.mean(xn * xn, axis=-1, keepdims=True) + EPS) * fg_ref[...]
    o_ref[...] = xn


def _outproj(a, b, x, mod, w, final_g, rows_per_group, tm, final):
    m = x.shape[0]
    r = mod.shape[1]
    return pl.pallas_call(
        functools.partial(_outproj_kernel, final=final),
        grid=(m // tm,),
        in_specs=[
            pl.BlockSpec((tm, HEAD_W), lambda i: (i, 0)),
            pl.BlockSpec((tm, HEAD_W), lambda i: (i, 0)),
            pl.BlockSpec((tm, D_MODEL), lambda i: (i, 0)),
            pl.BlockSpec((None, r, 3 * D_MODEL), lambda i: ((i * tm) // rows_per_group, 0, 0)),
            pl.BlockSpec((2 * HEAD_W, D_MODEL), lambda i: (0, 0)),
            pl.BlockSpec((1, D_MODEL), lambda i: (0, 0)),
        ],
        out_specs=pl.BlockSpec((tm, D_MODEL), lambda i: (i, 0)),
        out_shape=jax.ShapeDtypeStruct((m, D_MODEL), F32),
        name="outproj",
        compiler_params=_params(("arbitrary",)),
    )(a, b, x, mod, w, final_g.reshape(1, D_MODEL))


def _later_matrix(n):
    return jnp.where(_iota((n, n), 0) > _iota((n, n), 1), 1.0, 0.0).astype(BF16)


def _sb_rows_tile(z2, valid, carry, later):
    sp = jnp.maximum(z2, 0.0) + jnp.log2(1.0 + jnp.exp2(-jnp.abs(z2)))
    if valid is not None:
        sp = jnp.where(valid, sp, 0.0)
    hi = sp.astype(BF16)
    lo = (sp - hi.astype(F32)).astype(BF16)
    t = sp + jnp.dot(hi, later, preferred_element_type=F32) + jnp.dot(lo, later, preferred_element_type=F32) + carry
    w = jnp.exp2(z2 - t)
    if valid is not None:
        w = jnp.where(valid, w, 0.0)
    return w, t[:, 0:1]


def _sb_prompt_kernel(q_ref, z_ref, kt_hbm, vt_hbm, o_ref, kt_sc, vt_sc, sem, acc_sc,
                      qst_sc, z_sc, sp_sc, hilo_sc, carry_sc, *, tq):
    b = pl.program_id(0)
    i = pl.program_id(1)

    @pl.when(i == 0)
    def _():
        ck = pltpu.make_async_copy(kt_hbm.at[b], kt_sc, sem.at[0])
        cv = pltpu.make_async_copy(vt_hbm.at[b], vt_sc, sem.at[1])
        ck.start()
        cv.start()
        ck.wait()
        cv.wait()

    n_pairs = SB_HEADS // 2
    rows_all = SB_HEADS * tq
    later = _later_matrix(tq)
    lane_head = _iota((tq, PAIR_W), 1) // SB_DIM
    row_head = _iota((PAIR_W, tq), 0) // SB_DIM
    q = q_ref[...] * (SB_DIM ** -0.5 * LOG2E)
    for p in range(n_pairs):
        qp = q[:, p * PAIR_W:(p + 1) * PAIR_W]
        for hh in range(2):
            qst_sc[p, hh * tq:(hh + 1) * tq, :] = jnp.where(lane_head == hh, qp, 0.0).astype(BF16)
    strictly_before = _iota((2 * tq, tq), 1) < _iota((2 * tq, tq), 0) % tq

    def tile(j, diag):
        start = pl.multiple_of(j * tq, tq)
        for p in range(n_pairs):
            rs = slice(p * 2 * tq, (p + 1) * 2 * tq)
            ktp = kt_sc[p * PAIR_W:(p + 1) * PAIR_W, pl.ds(start, tq)].astype(BF16)
            z2 = jnp.dot(qst_sc[p], ktp, preferred_element_type=F32)
            sp = jnp.maximum(z2, 0.0) + jnp.log2(1.0 + jnp.exp2(-jnp.abs(z2)))
            if diag:
                sp = jnp.where(strictly_before, sp, 0.0)
            hi = sp.astype(BF16)
            z_sc[rs, :] = z2
            sp_sc[rs, :] = sp
            hilo_sc[rs, :] = hi
            hilo_sc[p * 2 * tq + rows_all:(p + 1) * 2 * tq + rows_all, :] = (sp - hi.astype(F32)).astype(BF16)
        cum = jnp.dot(hilo_sc[...], later, preferred_element_type=F32)
        for p in range(n_pairs):
            rs = slice(p * 2 * tq, (p + 1) * 2 * tq)
            t = sp_sc[rs, :] + cum[rs] + cum[p * 2 * tq + rows_all:(p + 1) * 2 * tq + rows_all]
            if not diag:
                t = t + carry_sc[rs, :]
            w = jnp.exp2(z_sc[rs, :] - t)
            if diag:
                w = jnp.where(strictly_before, w, 0.0)
            carry_sc[rs, :] = t[:, 0:1]
            vtp = vt_sc[p * PAIR_W:(p + 1) * PAIR_W, pl.ds(start, tq)]
            w2 = jnp.concatenate([w[0:tq], w[tq:2 * tq]], axis=1)
            v2 = jnp.concatenate([jnp.where(row_head == hh, vtp, 0.0) for hh in range(2)], axis=1)
            upd = _dot_nt(w2, v2)
            if diag:
                acc_sc[:, p * PAIR_W:(p + 1) * PAIR_W] = upd
            else:
                acc_sc[:, p * PAIR_W:(p + 1) * PAIR_W] += upd
        return jnp.min(carry_sc[...])

    least = tile(i, True)

    def cond(c):
        return (c[0] >= 0) & (c[1] < SB_SKIP2)

    def body(c):
        return c[0] - 1, tile(c[0], False)

    lax.while_loop(cond, body, (i - 1, least))
    o_ref[...] = acc_sc[...] * _silu(z_ref[...])


def _sb_prompt(q, kt, vt, z, tq=128):
    b, s, _ = q.shape
    blk = pl.BlockSpec((None, tq, HEAD_W), lambda bi, i: (bi, i, 0))
    hbm = pl.BlockSpec(memory_space=pl.ANY)
    return pl.pallas_call(
        functools.partial(_sb_prompt_kernel, tq=tq),
        grid=(b, s // tq),
        in_specs=[blk, blk, hbm, hbm],
        out_specs=blk,
        out_shape=jax.ShapeDtypeStruct((b, s, HEAD_W), F32),
        scratch_shapes=[pltpu.VMEM((HEAD_W, s), F32),
                        pltpu.VMEM((HEAD_W, s), F32),
                        pltpu.SemaphoreType.DMA((2,)),
                        pltpu.VMEM((tq, HEAD_W), F32),
                        pltpu.VMEM((SB_HEADS // 2, 2 * tq, PAIR_W), BF16),
                        pltpu.VMEM((SB_HEADS * tq, tq), F32),
                        pltpu.VMEM((SB_HEADS * tq, tq), F32),
                        pltpu.VMEM((2 * SB_HEADS * tq, tq), BF16),
                        pltpu.VMEM((SB_HEADS * tq, 1), F32)],
        name="sb_prompt",
        compiler_params=_params(("arbitrary", "arbitrary"), 48),
    )(q, z, kt, vt)


def _tile_rows(x, reps):
    return jnp.concatenate([x] * reps, axis=0)


def _block_diag_queries(q, n_cols, head_dim):
    qt = _tile_rows(q, n_cols // SAMPLE_PAD)
    row_head = _iota((n_cols, HEAD_W), 0) // SAMPLE_PAD
    lane_head = _iota((n_cols, HEAD_W), 1) // head_dim
    return jnp.where(row_head == lane_head, qt, 0.0)


def _gather_heads(res, head_dim):
    lane_head = _iota((SAMPLE_PAD, HEAD_W), 1) // head_dim
    out = jnp.zeros((SAMPLE_PAD, HEAD_W), F32)
    for h in range(HEAD_W // head_dim):
        out = out + jnp.where(lane_head == h, res[h * SAMPLE_PAD:(h + 1) * SAMPLE_PAD, :], 0.0)
    return out


def _sb_sample_kernel(pt_ref, q_ref, kn_ref, vn_ref, z_ref, ck_hbm, cv_hbm, o_ref,
                      kbuf, vbuf, sem, acc_ref, kpad, vpad, *, n_pages, t_real):
    b = pl.program_id(0)
    nq = PAIR_W

    def copies(p, slot):
        phys = pt_ref[b * n_pages + p]
        return (pltpu.make_async_copy(ck_hbm.at[phys], kbuf.at[slot], sem.at[0, slot]),
                pltpu.make_async_copy(cv_hbm.at[phys], vbuf.at[slot], sem.at[1, slot]))

    def start(p, slot):
        for cp in copies(p, slot):
            cp.start()

    def wait(p, slot):
        for cp in copies(p, slot):
            cp.wait()

    start(n_pages - 1, (n_pages - 1) % 2)
    start(n_pages - 2, (n_pages - 2) % 2)

    @pl.when(b == 0)
    def _():
        kpad[...] = jnp.zeros_like(kpad)
        vpad[...] = jnp.zeros_like(vpad)

    kpad[0:SAMPLE_PAD, :] = kn_ref[...]
    vpad[0:SAMPLE_PAD, :] = vn_ref[...]

    qbd = _block_diag_queries(q_ref[...] * (SB_DIM ** -0.5 * LOG2E), nq, SB_DIM).astype(BF16)
    later = _later_matrix(PAGE)
    qrow = _iota((nq, PAGE), 0)
    key = _iota((nq, PAGE), 1)
    valid_new = (key < qrow % SAMPLE_PAD) & (key < t_real)
    row1 = _iota((nq, 1), 0)
    real_row = (row1 < SB_HEADS * SAMPLE_PAD) & (row1 % SAMPLE_PAD < t_real)

    w, carry = _sb_rows_tile(_dot_nt(qbd, kpad[...]), valid_new, jnp.zeros((nq, 1), F32), later)
    acc_ref[...] = _dot(w, vpad[...])

    def least(c):
        return jnp.min(jnp.where(real_row, c, jnp.inf))

    def cond(c):
        return (c[0] >= 0) & (c[2] < SB_SKIP2)

    def body(c):
        p, carry, _ = c
        slot = p % 2
        wait(p, slot)
        w, cn = _sb_rows_tile(jnp.dot(qbd, kbuf[slot].astype(BF16), preferred_element_type=F32), None, carry, later)
        acc_ref[...] += _dot_nt(w, vbuf[slot])

        @pl.when(p >= 2)
        def _():
            start(p - 2, slot)

        return p - 1, cn, least(cn)

    p_end, _, _ = lax.while_loop(cond, body, (jnp.int32(n_pages - 1), carry, least(carry)))

    @pl.when(p_end >= 0)
    def _():
        wait(p_end, p_end % 2)

    @pl.when(p_end >= 1)
    def _():
        wait(p_end - 1, (p_end - 1) % 2)

    o_ref[...] = _gather_heads(acc_ref[...], SB_DIM) * _silu(z_ref[...])


def _sb_sample(q, kn, vn, z, cache_k, cache_v, page_table, t_real):
    nb = q.shape[0]
    n_pages = page_table.shape[1]
    blk = pl.BlockSpec((None, SAMPLE_PAD, HEAD_W), lambda bi, pt: (bi, 0, 0))
    hbm = pl.BlockSpec(memory_space=pl.ANY)
    return pl.pallas_call(
        functools.partial(_sb_sample_kernel, n_pages=n_pages, t_real=t_real),
        grid_spec=pltpu.PrefetchScalarGridSpec(
            num_scalar_prefetch=1,
            grid=(nb,),
            in_specs=[blk, blk, blk, blk, hbm, hbm],
            out_specs=blk,
            scratch_shapes=[
                pltpu.VMEM((2, HEAD_W, PAGE), F32),
                pltpu.VMEM((2, HEAD_W, PAGE), F32),
                pltpu.SemaphoreType.DMA((2, 2)),
                pltpu.VMEM((PAIR_W, HEAD_W), F32),
                pltpu.VMEM((PAGE, HEAD_W), F32),
                pltpu.VMEM((PAGE, HEAD_W), F32),
            ],
        ),
        out_shape=jax.ShapeDtypeStruct((nb, SAMPLE_PAD, HEAD_W), F32),
        name="sb_sample",
        compiler_params=_params(("arbitrary",)),
    )(page_table.reshape(-1), q, kn, vn, z, cache_k, cache_v)


def _mlstm_kernel(q_ref, k_ref, v_ref, og_ref, z_ref, gate_ref, bias_ref, gn_ref, c0_ref, n0_ref, m0_ref,
                  out_ref, c_out, n_out, m_out, c_sc, n_sc, m_sc, *, chunk, valid, per_step):
    ci = pl.program_id(1)

    @pl.when(ci == 0)
    def _():
        c_sc[...] = c0_ref[...]
        n_sc[...] = n0_ref[...]
        m_sc[...] = m0_ref[...]

    for bb in range(per_step):
        _mlstm_chunk(q_ref.at[bb], k_ref.at[bb], v_ref.at[bb], og_ref.at[bb], z_ref.at[bb], gate_ref.at[bb],
                     bias_ref, gn_ref, out_ref.at[bb], c_sc.at[bb], n_sc.at[bb], m_sc.at[bb], chunk, valid)

    @pl.when(ci == pl.num_programs(1) - 1)
    def _():
        c_out[...] = c_sc[...]
        n_out[...] = n_sc[...]
        m_out[...] = m_sc[...]


def _mlstm_chunk(q_ref, k_ref, v_ref, og_ref, z_ref, gate_ref, bias_ref, gn_ref, out_ref, c_sc, n_sc, m_sc,
                 chunk, valid):
    pre = gate_ref[...] + bias_ref[...]
    lf = pltpu.roll(_log_sigmoid(pre), PAIR_W - ML_HEADS, 1)
    ig = pre
    if valid < chunk:
        live = _iota((chunk, PAIR_W), 0) < valid
        lf = jnp.where(live, lf, 0.0)
        ig = jnp.where(live, ig, -jnp.inf)
    r = _iota((chunk, chunk), 0)
    c = _iota((chunk, chunk), 1)
    tril = c <= r
    lf_hi = lf.astype(BF16)
    bc2 = jnp.dot(jnp.where(tril, 1.0, 0.0).astype(BF16),
                  jnp.concatenate([lf_hi, (lf - lf_hi.astype(F32)).astype(BF16)], axis=1), preferred_element_type=F32)
    bc = bc2[:, :PAIR_W] + bc2[:, PAIR_W:]
    a = ig - bc
    a_t = a.T
    q_all = q_ref[...]
    k_all = k_ref[...] * (ML_DIM ** -0.5)
    v_all = v_ref[...]
    rows_st = ML_HEADS * chunk
    q_st = jnp.where(_iota((rows_st, HEAD_W), 0) // chunk == _iota((rows_st, HEAD_W), 1) // ML_DIM,
                     jnp.concatenate([q_all] * ML_HEADS, axis=0), 0.0).astype(BF16)
    qk_st = _dot_nt(q_st, k_all)
    qc_st = _dot(q_st, c_sc[...].reshape(ML_HEADS * ML_DIM, ML_DIM))
    wts, ms, inters, wks, decays = [], [], [], [], []
    for h in range(ML_HEADS):
        sl = slice(h * ML_DIM, (h + 1) * ML_DIM)
        bc_h = bc[:, h:h + 1]
        a_row = a_t[h:h + 1, :]
        m0 = m_sc[h:h + 1, 0:1]
        cm = jnp.max(jnp.where(tril, a_row, -jnp.inf), axis=-1, keepdims=True)
        m_h = jnp.maximum(bc_h + m0, cm + bc_h)
        dmat = jnp.exp(jnp.where(tril, bc_h - m_h + a_row, -jnp.inf))
        wts.append(dmat * qk_st[h * chunk:(h + 1) * chunk])
        ms.append(m_h)
        inters.append(jnp.exp(bc_h + m0 - m_h))
        bc_l = bc_h[chunk - 1:chunk, :]
        m_l = m_h[chunk - 1:chunk, :]
        decays.append(jnp.exp(bc_l + m0 - m_l))
        wks.append(jnp.exp(bc_l + a[:, h:h + 1] - m_l) * k_all[:, sl])
        m_sc[h:h + 1, :] = jnp.broadcast_to(m_l, (1, PAIR_W))
    wv_st = _dot(jnp.concatenate(wts, axis=0), v_all)
    wk_all = jnp.concatenate(wks, axis=1)
    upd = _dot_tn(wk_all, v_all)
    for h in range(ML_HEADS):
        sl = slice(h * ML_DIM, (h + 1) * ML_DIM)
        rs = slice(h * chunk, (h + 1) * chunk)
        n_h = n_sc[h:h + 1, :]
        num = inters[h] * qc_st[rs] + wv_st[rs, sl]
        den = (inters[h] * jnp.sum(q_all[:, sl] * n_h, axis=-1, keepdims=True)
               + jnp.sum(wts[h], axis=-1, keepdims=True))
        hv = num / jnp.maximum(jnp.abs(den), jnp.exp(-ms[h]))
        hn = hv * lax.rsqrt(jnp.mean(hv * hv, axis=-1, keepdims=True) + EPS) * gn_ref[:, sl]
        out_ref[:, sl] = hn * _sigmoid(og_ref[:, sl]) * _silu(z_ref[:, sl])
        c_sc[h] = decays[h] * c_sc[h] + upd[sl, sl]
        n_sc[h:h + 1, :] = decays[h] * n_h + jnp.sum(wks[h], axis=0, keepdims=True)


ML_PER_STEP = 1


def _mlstm(q, k, v, og, z, gates, bias, gn, c0, n0, m0, nb, chunk, valid):
    m = q.shape[0]
    t = m // nb
    n_chunks = t // chunk
    per = ML_PER_STEP
    tok = pl.BlockSpec((per, chunk, HEAD_W), lambda g, ci: (g, ci, 0))
    st_c = pl.BlockSpec((per, ML_HEADS, ML_DIM, ML_DIM), lambda g, ci: (g, 0, 0, 0))
    st_v = pl.BlockSpec((per, SAMPLE_PAD, ML_DIM), lambda g, ci: (g, 0, 0))
    r3 = lambda u: u.reshape(nb, t, u.shape[-1])
    out, c1, n1, m1 = pl.pallas_call(
        functools.partial(_mlstm_kernel, chunk=chunk, valid=valid, per_step=per),
        grid=(nb // per, n_chunks),
        in_specs=[tok, tok, tok, tok, tok,
                  pl.BlockSpec((per, chunk, PAIR_W), lambda g, ci: (g, ci, 0)),
                  pl.BlockSpec((1, PAIR_W), lambda g, ci: (0, 0)),
                  pl.BlockSpec((1, HEAD_W), lambda g, ci: (0, 0)),
                  st_c, st_v, st_v],
        out_specs=[tok, st_c, st_v, st_v],
        out_shape=[jax.ShapeDtypeStruct((nb, t, HEAD_W), F32),
                   jax.ShapeDtypeStruct((nb, ML_HEADS, ML_DIM, ML_DIM), F32),
                   jax.ShapeDtypeStruct((nb, SAMPLE_PAD, ML_DIM), F32),
                   jax.ShapeDtypeStruct((nb, SAMPLE_PAD, ML_DIM), F32)],
        scratch_shapes=[pltpu.VMEM((per, ML_HEADS, ML_DIM, ML_DIM), F32),
                        pltpu.VMEM((per, SAMPLE_PAD, ML_DIM), F32),
                        pltpu.VMEM((per, SAMPLE_PAD, ML_DIM), F32)],
        name="mlstm",
        compiler_params=_params(("arbitrary", "arbitrary")),
    )(r3(q), r3(k), r3(v), r3(og), r3(z), r3(gates), bias, gn, c0, n0, m0)
    return out.reshape(m, HEAD_W), c1, n1, m1


def _gla_kernel(q_ref, k_ref, v_ref, z_ref, gg_ref, wg_ref, bg_ref, gn_ref, s0_ref, out_ref, s_out,
                st_sc, kb, gb, vb, qa_sc, kc_sc, dec_sc, oi_sc, *, tb, sub, valid):
    ti = pl.program_id(1)
    kw = GLA_HEADS * GLA_DK

    @pl.when(ti == 0)
    def _():
        for p in range(2):
            st_sc[p] = s0_ref[p].T
        kb[0:sub, :] = jnp.zeros((sub, kw), F32)
        gb[0:sub, :] = jnp.zeros((sub, kw), F32)
        vb[0:sub, :] = jnp.zeros((sub, HEAD_W), F32)

    lg = _log_sigmoid(_dot(gg_ref[...], wg_ref[...]) + bg_ref[...]) * (1.0 / GLA_TAU)
    k = k_ref[...]
    if valid < tb:
        live = _iota((tb, kw), 0) < valid
        lg = jnp.where(live, lg, 0.0)
        k = jnp.where(live, k, 0.0)
    r = _iota((tb, tb), 0)
    c = _iota((tb, tb), 1)
    same = (r // sub) == (c // sub)
    sums = jnp.concatenate([jnp.where(same & (c <= r), 1.0, 0.0), jnp.where(same, 1.0, 0.0)], axis=0).astype(BF16)
    lg_hi = lg.astype(BF16)
    parts = jnp.dot(sums, jnp.concatenate([lg_hi, (lg - lg_hi.astype(F32)).astype(BF16)], axis=1),
                    preferred_element_type=F32)
    g = parts[:tb, :kw] + parts[:tb, kw:]
    gend = parts[tb:, :kw] + parts[tb:, kw:]
    q = q_ref[...] * (GLA_DK ** -0.5)
    v = v_ref[...]
    qa_sc[...] = q * jnp.exp(g)
    kc_sc[...] = k * jnp.exp(gend - g)
    dec_sc[...] = jnp.exp(gend)
    kb[sub:sub + tb, :] = k
    gb[sub:sub + tb, :] = g
    vb[sub:sub + tb, :] = v

    head_spread = jnp.where(_iota((kw, HEAD_W), 0) // GLA_DK == _iota((kw, HEAD_W), 1) // GLA_DV, 1.0, 0.0).astype(BF16)
    rmod = _iota((tb, 1), 0) % sub
    oi = jnp.zeros((tb, HEAD_W), F32)
    for d in range(sub):
        lo = sub - d
        dec = jnp.exp(jnp.where(rmod >= d, g - gb[lo:lo + tb, :], NEG))
        oi = oi + _dot(q * kb[lo:lo + tb, :] * dec, head_spread) * vb[lo:lo + tb, :]
    oi_sc[...] = oi

    lane_head = _iota((2 * sub, PAIR_W), 1) // GLA_DK
    row_head = _iota((2 * sub, PAIR_W), 0) // sub
    for i in range(tb // sub):
        rows = slice(i * sub, (i + 1) * sub)
        qa = qa_sc[rows, :]
        kc = kc_sc[rows, :]
        dec = dec_sc[rows, :][0:1, :]
        for p in range(2):
            ps = slice(p * PAIR_W, (p + 1) * PAIR_W)
            vs = slice(2 * p * GLA_DV, (2 * p + 2) * GLA_DV)
            st = st_sc[p]
            own = lane_head == row_head
            o2 = _dot_nt(jnp.where(own, jnp.concatenate([qa[:, ps]] * 2, axis=0), 0.0), st)
            v2 = jnp.concatenate([v_ref[rows, vs][:, :GLA_DV], v_ref[rows, vs][:, GLA_DV:]], axis=0)
            st_sc[p] = st * dec[:, ps] + _dot_tn(v2, jnp.where(own, jnp.concatenate([kc[:, ps]] * 2, axis=0), 0.0))
            for hh in range(2):
                hs = slice((2 * p + hh) * GLA_DV, (2 * p + hh + 1) * GLA_DV)
                o = o2[hh * sub:(hh + 1) * sub] + oi_sc[rows, hs]
                on = o * lax.rsqrt(jnp.mean(o * o, axis=-1, keepdims=True) + EPS) * gn_ref[:, hs]
                out_ref[rows, hs] = on * _silu(z_ref[rows, hs])

    @pl.when(ti == pl.num_programs(1) - 1)
    def _():
        for p in range(2):
            s_out[p] = st_sc[p].T


def _gla(q, k, v, z, gg, wg, bg, gn, s0, nb, tb, sub, valid):
    m = q.shape[0]
    nt = m // (nb * tb)
    kw = GLA_HEADS * GLA_DK
    tok_k = pl.BlockSpec((tb, kw), lambda b, t: (b * nt + t, 0))
    tok_v = pl.BlockSpec((tb, HEAD_W), lambda b, t: (b * nt + t, 0))
    st = pl.BlockSpec((None, 2, PAIR_W, GLA_DV), lambda b, t: (b, 0, 0, 0))
    return pl.pallas_call(
        functools.partial(_gla_kernel, tb=tb, sub=sub, valid=valid),
        grid=(nb, nt),
        in_specs=[tok_k, tok_k, tok_v, tok_v,
                  pl.BlockSpec((tb, PAIR_W), lambda b, t: (b * nt + t, 0)),
                  pl.BlockSpec((PAIR_W, kw), lambda b, t: (0, 0)),
                  pl.BlockSpec((1, kw), lambda b, t: (0, 0)),
                  pl.BlockSpec((1, HEAD_W), lambda b, t: (0, 0)),
                  st],
        out_specs=[tok_v, st],
        out_shape=[jax.ShapeDtypeStruct((m, HEAD_W), F32),
                   jax.ShapeDtypeStruct((nb, 2, PAIR_W, GLA_DV), F32)],
        scratch_shapes=[pltpu.VMEM((2, GLA_DV, PAIR_W), F32),
                        pltpu.VMEM((sub + tb, kw), F32),
                        pltpu.VMEM((sub + tb, kw), F32),
                        pltpu.VMEM((sub + tb, HEAD_W), F32),
                        pltpu.VMEM((tb, kw), F32),
                        pltpu.VMEM((tb, kw), F32),
                        pltpu.VMEM((tb, kw), F32),
                        pltpu.VMEM((tb, HEAD_W), F32)],
        name="gla",
        compiler_params=_params(("arbitrary", "arbitrary")),
    )(q, k, v, z, gg, wg, bg, gn, s0)


def _top3_rows(s, n_valid):
    nb = s.shape[0]
    nrow = _iota(s.shape, 0)
    live = nrow < n_valid
    s = jnp.where(live, s, -jnp.inf)
    chosen = jnp.zeros(s.shape, F32)
    picks = []
    for _ in range(MB_TOPK):
        mx = jnp.max(s, axis=0, keepdims=True)
        idx = jnp.min(jnp.where(s == mx, nrow, nb), axis=0, keepdims=True)
        pick = nrow == idx
        chosen = jnp.where(pick, jnp.where(live, 1.0, 0.0), chosen)
        s = jnp.where(pick, -jnp.inf, s)
        picks.append(idx)
    return picks, chosen


KEY_CHUNK = 64
SUM_ROWS = 16


def _moba_prompt_kernel(q_ref, kt_ref, vt_ref, z_ref, o_ref,
                        ktok_sc, vtbd_sc, km_sc, bias_sc, sel2_sc, sel_sc, qtm_sc, sa_sc, sb_sc, sd_sc,
                        cma_sc, cmb_sc, cmd_sc, p_sc, acc_sc, m_sc, *, tq, nblk):
    p = pl.program_id(1)
    i = pl.program_id(2)
    slope2 = [jnp.exp2(-(jnp.full((1, tq), 2 * p + hh + 1, jnp.int32)).astype(F32)) * LOG2E for hh in range(2)]

    row_head = _iota((PAIR_W, tq), 0) // MB_DIM

    @pl.when(i == 0)
    def _():
        lane = _iota((PAIR_W, PAIR_W), 1)
        feat_head = _iota((PAIR_W, MB_BLOCK), 0) // MB_DIM
        km = jnp.zeros((PAIR_W, PAIR_W), F32)
        for n in range(nblk):
            cols = slice(n * MB_BLOCK, (n + 1) * MB_BLOCK)
            blk = kt_ref[:, cols]
            ktok_sc[cols, :] = blk.T.astype(BF16)
            km = jnp.where(lane == n, jnp.mean(blk, axis=1, keepdims=True), km)
            vblk = vt_ref[:, cols]
            ones_row = _iota((SUM_ROWS, MB_BLOCK), 0)
            for hh in range(2):
                vtbd_sc[0:PAIR_W, (2 * n + hh) * MB_BLOCK:(2 * n + hh + 1) * MB_BLOCK] = (
                    jnp.where(feat_head == hh, vblk, 0.0).astype(BF16))
                vtbd_sc[PAIR_W:PAIR_W + SUM_ROWS, (2 * n + hh) * MB_BLOCK:(2 * n + hh + 1) * MB_BLOCK] = (
                    jnp.where(ones_row == hh, 1.0, 0.0).astype(BF16))
        km_t = km.T
        blk_head = _iota((PAIR_W, PAIR_W), 1) // MB_DIM
        for hh in range(2):
            km_sc[hh * nblk:(hh + 1) * nblk, :] = jnp.where(blk_head == hh, km_t, 0.0)[0:nblk]
        key_off = _iota((MB_BLOCK, tq), 0).astype(F32)
        for hh in range(2):
            bias_sc[hh] = slope2[hh] * key_off

    q_t = q_ref[...].T
    s_sel = jnp.dot(km_sc[...], q_t, precision=lax.Precision.HIGHEST, preferred_element_type=F32)
    for hh in range(2):
        _, chosen = _top3_rows(s_sel[hh * nblk:(hh + 1) * nblk], i)
        sel2_sc[...] = chosen
        for n in range(nblk):
            sel_sc[hh, n] = sel2_sc[n:n + 1, :]
        qtm_sc[:, hh * tq:(hh + 1) * tq] = jnp.where(row_head == hh, q_t * (MB_DIM ** -0.5 * LOG2E), 0.0).astype(BF16)

    n_chunks = MB_BLOCK // KEY_CHUNK

    bufs = ((sa_sc, cma_sc), (sb_sc, cmb_sc), (sd_sc, cmd_sc))
    acc_row = _iota((PAIR_W + SUM_ROWS, tq), 0)
    acc_head0 = (acc_row < MB_DIM) | (acc_row == PAIR_W)

    def scores(n, buf, diag):
        s_ref, cm_ref = bufs[buf]
        start = pl.multiple_of(n * MB_BLOCK, MB_BLOCK)
        s = jnp.dot(ktok_sc[pl.ds(start, MB_BLOCK), :], qtm_sc[...], preferred_element_type=F32)
        for hh in range(2):
            cm = None
            for kc in range(n_chunks):
                rows = slice(kc * KEY_CHUNK, (kc + 1) * KEY_CHUNK)
                sc = s[rows, hh * tq:(hh + 1) * tq] + bias_sc[hh, rows]
                if diag:
                    key = _iota((KEY_CHUNK, tq), 0) + kc * KEY_CHUNK
                    sc = jnp.where(key <= _iota((KEY_CHUNK, tq), 1), sc, NEG)
                s_ref[hh, rows] = sc
                c = jnp.max(sc, axis=0, keepdims=True)
                cm = c if cm is None else jnp.maximum(cm, c)
            cm_ref[hh] = cm

    def attend(n, buf, diag):
        s_ref, cm_ref = bufs[buf]
        alphas = []
        for hh in range(2):
            cm = cm_ref[hh]
            if diag:
                m_new = cm
                shift = cm
            else:
                c = slope2[hh] * ((n - i) * MB_BLOCK).astype(F32)
                ok = sel_sc[hh, n] > 0.5
                m_old = m_sc[hh]
                m_new = jnp.maximum(m_old, jnp.where(ok, cm + c, NEG))
                shift = jnp.where(ok, m_new - c, -NEG)
                alphas.append(jnp.exp2(m_old - m_new))
            for kc in range(n_chunks):
                rows = slice(kc * KEY_CHUNK, (kc + 1) * KEY_CHUNK)
                p_sc[hh * MB_BLOCK + kc * KEY_CHUNK:hh * MB_BLOCK + (kc + 1) * KEY_CHUNK, :] = (
                    jnp.exp2((s_ref[hh, rows] - shift).astype(BF16)))
            m_sc[hh] = m_new
        vt = vtbd_sc[:, pl.ds(pl.multiple_of(n * (2 * MB_BLOCK), 2 * MB_BLOCK), 2 * MB_BLOCK)]
        pv = jnp.dot(vt, p_sc[...], preferred_element_type=F32)
        if diag:
            acc_sc[...] = pv
        else:
            acc_sc[...] = acc_sc[...] * jnp.where(acc_head0, alphas[0], alphas[1]) + pv

    scores(i, 2, True)

    @pl.when(i > 0)
    def _():
        scores(0, 0, False)
        scores(1, 1, False)

    attend(i, 2, True)

    def body(k, _):
        n0 = 2 * k
        attend(n0, 0, False)
        scores(jnp.minimum(n0 + 2, nblk - 1), 0, False)
        attend(n0 + 1, 1, False)
        scores(jnp.minimum(n0 + 3, nblk - 1), 1, False)
        return 0

    lax.fori_loop(0, (i + 1) // 2, body, 0)
    inv = [1.0 / acc_sc[PAIR_W + hh:PAIR_W + hh + 1, :] for hh in range(2)]
    out_t = acc_sc[0:PAIR_W, :] * jnp.where(row_head == 0, inv[0], inv[1])
    o_ref[...] = out_t.T * _silu(z_ref[...])


def _moba_prompt(q, kt, vt, z):
    b, s, _ = q.shape
    tq = MB_BLOCK
    nblk = s // MB_BLOCK
    assert nblk <= PAIR_W
    blk_q = pl.BlockSpec((None, tq, PAIR_W), lambda bi, p, i: (bi, i, p))
    blk_kv = pl.BlockSpec((None, PAIR_W, s), lambda bi, p, i: (bi, p, 0))
    return pl.pallas_call(
        functools.partial(_moba_prompt_kernel, tq=tq, nblk=nblk),
        grid=(b, HEAD_W // PAIR_W, s // tq),
        in_specs=[blk_q, blk_kv, blk_kv, blk_q],
        out_specs=blk_q,
        out_shape=jax.ShapeDtypeStruct((b, s, HEAD_W), F32),
        scratch_shapes=[pltpu.VMEM((s, PAIR_W), BF16),
                        pltpu.VMEM((PAIR_W + SUM_ROWS, 2 * s), BF16),
                        pltpu.VMEM((2 * nblk, PAIR_W), F32),
                        pltpu.VMEM((2, MB_BLOCK, tq), F32),
                        pltpu.VMEM((nblk, tq), F32),
                        pltpu.VMEM((2, nblk, 1, tq), F32),
                        pltpu.VMEM((PAIR_W, 2 * tq), BF16),
                        pltpu.VMEM((2, MB_BLOCK, tq), F32),
                        pltpu.VMEM((2, MB_BLOCK, tq), F32),
                        pltpu.VMEM((2, MB_BLOCK, tq), F32),
                        pltpu.VMEM((2, 1, tq), F32),
                        pltpu.VMEM((2, 1, tq), F32),
                        pltpu.VMEM((2, 1, tq), F32),
                        pltpu.VMEM((2 * MB_BLOCK, tq), BF16),
                        pltpu.VMEM((PAIR_W + SUM_ROWS, tq), F32),
                        pltpu.VMEM((2, 1, tq), F32)],
        name="moba_prompt",
        compiler_params=_params(("arbitrary", "arbitrary", "arbitrary")),
    )(q, kt, vt, z)


_MEAN_PAGES = 32


def _top3_lanes(s, n_valid):
    width = s.shape[1]
    lane = _iota(s.shape, 1)
    s = jnp.where(lane < n_valid, s, -jnp.inf)
    picks = []
    for _ in range(MB_TOPK):
        mx = jnp.max(s, axis=1, keepdims=True)
        idx = jnp.min(jnp.where(s == mx, lane, width), axis=1, keepdims=True)
        s = jnp.where(lane == idx, -jnp.inf, s)
        picks.append(idx)
    return picks


def _mb_select_kernel(pt_ref, *refs, n_blocks):
    pages = refs[:_MEAN_PAGES]
    q_ref, idx_ref, km_sc = refs[_MEAN_PAGES:]
    s = pl.program_id(1)
    per_blk = MB_BLOCK // PAGE
    blk_per_step = _MEAN_PAGES // per_blk

    @pl.when(s == 0)
    def _():
        km_sc[...] = jnp.zeros_like(km_sc)

    lane = _iota((HEAD_W, PAIR_W), 1)
    km = km_sc[...]
    for j in range(blk_per_step):
        both = sum(pages[per_blk * j + e][...] for e in range(per_blk))
        km = jnp.where(lane == s * blk_per_step + j, jnp.sum(both, axis=1, keepdims=True) * (1.0 / MB_BLOCK), km)
    km_sc[...] = km

    @pl.when(s == pl.num_programs(1) - 1)
    def _():
        qbd = _block_diag_queries(q_ref[...], PAIR_W, MB_DIM)
        sc = jnp.dot(qbd, km, precision=lax.Precision.HIGHEST, preferred_element_type=F32)
        picks = _top3_lanes(sc, n_blocks)
        lane_o = _iota((PAIR_W, PAIR_W), 1)
        out = jnp.zeros((PAIR_W, PAIR_W), jnp.int32)
        for k, pk in enumerate(picks):
            out = jnp.where(lane_o == k, pk, out)
        idx_ref[...] = out


def _mb_select(q, cache_k, page_table):
    nb = q.shape[0]
    n_pages = page_table.shape[1]
    steps = n_pages // _MEAN_PAGES
    n_blocks = n_pages * PAGE // MB_BLOCK
    assert n_pages % _MEAN_PAGES == 0 and MB_TOPK <= n_blocks <= PAIR_W
    page_specs = [pl.BlockSpec((None, HEAD_W, PAGE), lambda bi, s, pt, j=j: (pt[bi * n_pages + s * _MEAN_PAGES + j], 0, 0))
                  for j in range(_MEAN_PAGES)]
    return pl.pallas_call(
        functools.partial(_mb_select_kernel, n_blocks=n_blocks),
        grid_spec=pltpu.PrefetchScalarGridSpec(
            num_scalar_prefetch=1,
            grid=(nb, steps),
            in_specs=page_specs + [pl.BlockSpec((None, SAMPLE_PAD, HEAD_W), lambda bi, s, pt: (bi, 0, 0))],
            out_specs=pl.BlockSpec((None, PAIR_W, PAIR_W), lambda bi, s, pt: (bi, 0, 0)),
            scratch_shapes=[pltpu.VMEM((HEAD_W, PAIR_W), F32)],
        ),
        out_shape=jax.ShapeDtypeStruct((nb, PAIR_W, PAIR_W), jnp.int32),
        name="moba_select",
        compiler_params=_params(("arbitrary", "arbitrary")),
    )(page_table.reshape(-1), *([cache_k] * _MEAN_PAGES), q)


def _mb_sample_kernel(pt_ref, sel_ref, q_ref, kn_ref, vn_ref, z_ref, idx_ref, ck_hbm, cv_hbm, o_ref,
                      kbuf, vbuf, sem, kpad, vpad, s_sc, *, n_pages, t_real):
    b = pl.program_id(0)
    per_blk = MB_BLOCK // PAGE
    n_grp = MB_TOPK * t_real
    n_sel = n_grp * MB_BLOCK
    past = n_pages * PAGE

    def copies(req, slot, h, k, t, j):
        blk = sel_ref[(req * MB_HEADS * t_real + h * t_real + t) * MB_TOPK + k]
        phys = pt_ref[req * n_pages + blk * per_blk + j]
        hs = pl.ds(h * MB_DIM, MB_DIM)
        ls = pl.ds(((k * t_real + t) * per_blk + j) * PAGE, PAGE)
        return (pltpu.make_async_copy(ck_hbm.at[phys, hs, :], kbuf.at[slot, hs, ls], sem.at[0, slot]),
                pltpu.make_async_copy(cv_hbm.at[phys, hs, :], vbuf.at[slot, hs, ls], sem.at[1, slot]))

    todo = [(h, k, t, j) for h in range(MB_HEADS) for k in range(MB_TOPK) for t in range(t_real)
            for j in range(per_blk)]

    def fetch(req, slot):
        for a in todo:
            for cp in copies(req, slot, *a):
                cp.start()

    slot = b % 2

    @pl.when(b == 0)
    def _():
        fetch(0, 0)
        kpad[...] = jnp.zeros_like(kpad)
        vpad[...] = jnp.zeros_like(vpad)

    @pl.when(b + 1 < pl.num_programs(0))
    def _():
        fetch(b + 1, 1 - slot)

    kpad[0:SAMPLE_PAD, :] = kn_ref[...]
    vpad[0:SAMPLE_PAD, :] = vn_ref[...]

    nq = PAIR_W
    qbd = _block_diag_queries(q_ref[...] * (MB_DIM ** -0.5), nq, MB_DIM).astype(BF16)
    row = _iota((nq, 1), 0)
    t_row = row % SAMPLE_PAD
    slope = jnp.exp2(-(row // SAMPLE_PAD + 1).astype(F32))

    key = _iota((nq, PAGE), 1)
    s_new = _dot_nt(qbd, kpad[...]) - slope * (t_row - key).astype(F32)
    s_new = jnp.where((key <= t_row) & (key < t_real), s_new, NEG)
    s_sc[:, n_sel:n_sel + PAGE] = s_new
    m = jnp.max(s_new, axis=1, keepdims=True)

    for a in todo:
        for cp in copies(b, slot, *a):
            cp.wait()

    key_in_blk = _iota((nq, MB_BLOCK), 1)
    for g in range(n_grp):
        k, t = divmod(g, t_real)
        lanes = slice(g * MB_BLOCK, (g + 1) * MB_BLOCK)
        pos = idx_ref[:, k:k + 1] * MB_BLOCK + key_in_blk
        sg = _dot(qbd, kbuf[slot, :, lanes]) - slope * (past + t_row - pos).astype(F32)
        sg = jnp.where(t_row == t, sg, NEG)
        s_sc[:, lanes] = sg
        m = jnp.maximum(m, jnp.max(sg, axis=1, keepdims=True))

    pr = jnp.exp(s_sc[...] - m)
    num = _dot_nt(pr[:, :n_sel], vbuf[slot]) + _dot(pr[:, n_sel:], vpad[...])
    res = num / jnp.sum(pr, axis=1, keepdims=True)
    o_ref[...] = _gather_heads(res, MB_DIM) * _silu(z_ref[...])


def _mb_sample(q, kn, vn, z, idx, cache_k, cache_v, page_table, t_real):
    nb = q.shape[0]
    n_pages = page_table.shape[1]
    n_sel = MB_TOPK * t_real * MB_BLOCK
    sel = idx[:, :MB_HEADS * SAMPLE_PAD, :MB_TOPK].reshape(nb, MB_HEADS, SAMPLE_PAD, MB_TOPK)[:, :, :t_real]
    blk = pl.BlockSpec((None, SAMPLE_PAD, HEAD_W), lambda bi, pt, sl: (bi, 0, 0))
    hbm = pl.BlockSpec(memory_space=pl.ANY)
    return pl.pallas_call(
        functools.partial(_mb_sample_kernel, n_pages=n_pages, t_real=t_real),
        grid_spec=pltpu.PrefetchScalarGridSpec(
            num_scalar_prefetch=2,
            grid=(nb,),
            in_specs=[blk, blk, blk, blk,
                      pl.BlockSpec((None, PAIR_W, PAIR_W), lambda bi, pt, sl: (bi, 0, 0)),
                      hbm, hbm],
            out_specs=blk,
            scratch_shapes=[pltpu.VMEM((2, HEAD_W, n_sel), F32),
                            pltpu.VMEM((2, HEAD_W, n_sel), F32),
                            pltpu.SemaphoreType.DMA((2, 2)),
                            pltpu.VMEM((PAGE, HEAD_W), F32),
                            pltpu.VMEM((PAGE, HEAD_W), F32),
                            pltpu.VMEM((PAIR_W, n_sel + PAGE), F32)],
        ),
        out_shape=jax.ShapeDtypeStruct((nb, SAMPLE_PAD, HEAD_W), F32),
        name="moba_sample",
        compiler_params=_params(("arbitrary",), 48),
    )(page_table.reshape(-1), sel.reshape(-1), q, kn, vn, z, idx, cache_k, cache_v)


def _pad_cols(w, width):
    return jnp.pad(w, ((0, 0), (0, width - w.shape[1])))


def _pages_feature_major(cache):
    n_pool, page, heads, dim = cache.shape
    return jnp.transpose(cache, (0, 2, 3, 1)).reshape(n_pool, heads * dim, page)


def kernel(x_prompt, x_sample, cache_sb_k, cache_sb_v, cache_mb_k, cache_mb_v, state_ml_C, state_ml_n, state_ml_m, state_gla_S, page_table, c_prompt, c_sample, w_mod, b_mod, norm_g, final_g, w_in_even, w_out_even, ml_b_i, ml_b_f, ml_gn, w_in_odd, w_out_odd, gla_w_g2, gla_b_g, gla_gn):
    bsz, seq, _ = x_prompt.shape
    db, t_real, _ = x_sample.shape
    mp = bsz * seq
    ms = db * SAMPLE_PAD

    rows = -(-(bsz + db) // 8) * 8
    c_all = jnp.pad(jnp.concatenate([c_prompt, c_sample], axis=0), ((0, rows - bsz - db), (0, 0)))
    mod = _modulation(c_all, w_mod, b_mod)
    mod_p = mod[:, :bsz, None, :]
    mod_s = jnp.repeat(mod[:, bsz:bsz + db], SAMPLE_PAD, axis=1)[:, None]

    xp = x_prompt.reshape(mp, D_MODEL)
    xs = jnp.pad(x_sample, ((0, 0), (0, SAMPLE_PAD - t_real), (0, 0))).reshape(ms, D_MODEL)
    tm = 256

    w_even = jnp.concatenate([w_in_even[0, :, :9 * HEAD_W], _pad_cols(w_in_even[0, :, 9 * HEAD_W:], PAIR_W)],
                             axis=1).astype(BF16)
    widths_even = (HEAD_W,) * 9 + (PAIR_W,)
    gate_bias = _pad_cols(jnp.concatenate([ml_b_i[0], ml_b_f[0]])[None, :], PAIR_W)
    ml_gn0 = ml_gn[0].reshape(1, HEAD_W)
    w_out0 = w_out_even[0].astype(BF16)

    w_even_tok = jnp.concatenate([w_even[:, :HEAD_W], w_even[:, 3 * HEAD_W:]], axis=1)
    w_even_kv = w_even[:, HEAD_W:3 * HEAD_W].T
    sq, sz, mq, mk, mv, mo, mz, mg, sk, sv = _inproj(xp, mod_p[0], norm_g[0], w_even_tok, (HEAD_W,) * 7 + (PAIR_W,),
                                                     seq, tm, w_even_kv, (HEAD_W, HEAD_W))
    rp = lambda u: u.reshape(bsz, seq, HEAD_W)
    a_p = _sb_prompt(rp(sq), sk, sv, rp(sz)).reshape(mp, HEAD_W)
    zc = jnp.zeros((bsz, ML_HEADS, ML_DIM, ML_DIM), F32)
    zv = jnp.zeros((bsz, SAMPLE_PAD, ML_DIM), F32)
    b_p, mlc_p, mln_p, mlm_p = _mlstm(mq, mk, mv, mo, mz, mg, gate_bias, ml_gn0, zc, zv, zv, bsz, ML_CHUNK, ML_CHUNK)
    x1p = _outproj(a_p, b_p, xp, mod_p[0], w_out0, final_g, seq, tm, False)
    sbk_p, sbv_p = sk, sv

    sq, sk, sv, sz, mq, mk, mv, mo, mz, mg = _inproj(xs, mod_s[0], norm_g[0], w_even, widths_even, ms, ms)
    r3 = lambda u: u.reshape(db, SAMPLE_PAD, HEAD_W)
    a_s = _sb_sample(r3(sq), r3(sk), r3(sv), r3(sz), _pages_feature_major(cache_sb_k[0]),
                     _pages_feature_major(cache_sb_v[0]), page_table, t_real).reshape(ms, HEAD_W)
    n0 = jnp.pad(state_ml_n[0], ((0, 0), (0, SAMPLE_PAD - ML_HEADS), (0, 0)))
    m0 = jnp.broadcast_to(jnp.pad(state_ml_m[0], ((0, 0), (0, SAMPLE_PAD - ML_HEADS)))[:, :, None],
                          (db, SAMPLE_PAD, ML_DIM))
    b_s, mlc_s, mln_s, mlm_s = _mlstm(mq, mk, mv, mo, mz, mg, gate_bias, ml_gn0, state_ml_C[0], n0, m0,
                                      db, SAMPLE_PAD, t_real)
    x1s = _outproj(a_s, b_s, xs, mod_s[0], w_out0, final_g, ms, ms, False)
    sbk_s, sbv_s = r3(sk)[:, :t_real], r3(sv)[:, :t_real]

    kw = GLA_HEADS * GLA_DK
    g0 = 2 * kw + 2 * HEAD_W
    w_odd = jnp.concatenate([w_in_odd[0, :, :g0], w_in_odd[0, :, g0 + GLA_RANK:],
                             _pad_cols(w_in_odd[0, :, g0:g0 + GLA_RANK], PAIR_W)], axis=1).astype(BF16)
    widths_odd = (kw, kw) + (HEAD_W,) * 6 + (PAIR_W,)
    wg = jnp.pad(gla_w_g2[0], ((0, PAIR_W - GLA_RANK), (0, 0))).astype(BF16)
    bg = gla_b_g[0].reshape(1, kw)
    gla_gn0 = gla_gn[0].reshape(1, HEAD_W)
    w_out1 = w_out_odd[0].astype(BF16)

    c0 = 2 * kw + 3 * HEAD_W
    w_odd_tok = jnp.concatenate([w_odd[:, :c0], w_odd[:, c0 + 2 * HEAD_W:]], axis=1)
    w_odd_kv = w_odd[:, c0:c0 + 2 * HEAD_W].T
    gq, gk, gv, gz, dq, dz, gg, dk, dv = _inproj(x1p, mod_p[1], norm_g[1], w_odd_tok,
                                                 (kw, kw) + (HEAD_W,) * 4 + (PAIR_W,), seq, tm,
                                                 w_odd_kv, (HEAD_W, HEAD_W))
    s_zero = jnp.zeros((bsz, 2, PAIR_W, GLA_DV), F32)
    c_p, glas_p = _gla(gq, gk, gv, gz, gg, wg, bg, gla_gn0, s_zero, bsz, 256, GLA_SUB, 256)
    d_p = _moba_prompt(rp(dq), dk, dv, rp(dz)).reshape(mp, HEAD_W)
    y_p = _outproj(c_p, d_p, x1p, mod_p[1], w_out1, final_g, seq, tm, True)
    mbk_p, mbv_p = dk, dv

    gq, gk, gv, gz, dq, dk, dv, dz, gg = _inproj(x1s, mod_s[1], norm_g[1], w_odd, widths_odd, ms, ms)
    c_s, glas_s = _gla(gq, gk, gv, gz, gg, wg, bg, gla_gn0, state_gla_S[0].reshape(db, 2, PAIR_W, GLA_DV),
                       db, SAMPLE_PAD, SAMPLE_PAD, t_real)
    ck = _pages_feature_major(cache_mb_k[0])
    cv = _pages_feature_major(cache_mb_v[0])
    idx = _mb_select(r3(dq), ck, page_table)
    d_s = _mb_sample(r3(dq), r3(dk), r3(dv), r3(dz), idx, ck, cv, page_table, t_real).reshape(ms, HEAD_W)
    y_s = _outproj(c_s, d_s, x1s, mod_s[1], w_out1, final_g, ms, ms, True)
    mbk_s, mbv_s = r3(dk)[:, :t_real], r3(dv)[:, :t_real]

    kv_p = lambda u, h, d: jnp.transpose(u.reshape(1, bsz, h, d, seq), (0, 1, 4, 2, 3))
    kv_s = lambda u, h, d: u.reshape(1, db, t_real, h, d)
    return (y_p.reshape(bsz, seq, D_MODEL), y_s.reshape(db, SAMPLE_PAD, D_MODEL)[:, :t_real],
            kv_p(sbk_p, SB_HEADS, SB_DIM), kv_p(sbv_p, SB_HEADS, SB_DIM),
            kv_s(sbk_s, SB_HEADS, SB_DIM), kv_s(sbv_s, SB_HEADS, SB_DIM),
            mlc_p[None], mln_p[None, :, :ML_HEADS], mlm_p[None, :, :ML_HEADS, 0],
            mlc_s[None], mln_s[None, :, :ML_HEADS], mlm_s[None, :, :ML_HEADS, 0],
            glas_p.reshape(1, bsz, GLA_HEADS, GLA_DK, GLA_DV), glas_s.reshape(1, db, GLA_HEADS, GLA_DK, GLA_DV),
            kv_p(mbk_p, MB_HEADS, MB_DIM), kv_p(mbv_p, MB_HEADS, MB_DIM),
            kv_s(mbk_s, MB_HEADS, MB_DIM), kv_s(mbv_s, MB_HEADS, MB_DIM))
```

```python
import functools

import jax
import jax.numpy as jnp
from jax import lax
from jax.experimental import pallas as pl
from jax.experimental.pallas import tpu as pltpu

F32 = jnp.float32
BF16 = jnp.bfloat16

D_MODEL = 1024
EPS = 1e-6
PAGE = 128
SB_HEADS, SB_DIM = 8, 64
ML_HEADS, ML_DIM = 4, 128
ML_CHUNK = 128
GLA_HEADS, GLA_DK, GLA_DV, GLA_RANK, GLA_TAU = 4, 64, 128, 16, 16.0
MB_HEADS, MB_DIM, MB_BLOCK, MB_TOPK = 8, 64, 256, 3
HEAD_W = 512
PAIR_W = 128
SAMPLE_PAD = 8
NEG = -1e30
LOG2E = 1.4426950408889634
SB_SKIP = 88.0
SB_SKIP2 = SB_SKIP * 1.4426950408889634
GLA_SUB = 16


def _dot(a, b):
    return jnp.dot(a.astype(BF16), b.astype(BF16), preferred_element_type=F32)


def _dot_nt(a, b):
    return lax.dot_general(a.astype(BF16), b.astype(BF16), (((1,), (1,)), ((), ())), preferred_element_type=F32)


def _dot_tn(a, b):
    return lax.dot_general(a.astype(BF16), b.astype(BF16), (((0,), (0,)), ((), ())), preferred_element_type=F32)


def _softplus_tail(z):
    return jnp.log(1.0 + jnp.exp(-jnp.abs(z)))


def _log_sigmoid(z):
    return jnp.minimum(z, 0.0) - _softplus_tail(z)


def _sigmoid(z):
    return 1.0 / (1.0 + jnp.exp(-z))


def _silu(z):
    return z * _sigmoid(z)


def _iota(shape, axis):
    return lax.broadcasted_iota(jnp.int32, shape, axis)


def _params(sem, vmem_mb=None):
    kw = dict(dimension_semantics=sem)
    if vmem_mb is not None:
        kw["vmem_limit_bytes"] = vmem_mb * 1024 * 1024
    return pltpu.CompilerParams(**kw)


def _mod_kernel(c_ref, w_ref, b_ref, o_ref):
    o_ref[...] = _dot(_silu(c_ref[...]), w_ref[...]) + b_ref[...]


def _modulation(c_all, w_mod, b_mod):
    depth = w_mod.shape[0]
    rows = c_all.shape[0]
    return pl.pallas_call(
        _mod_kernel,
        grid=(depth, 3),
        in_specs=[
            pl.BlockSpec((rows, D_MODEL), lambda l, j: (0, 0)),
            pl.BlockSpec((None, D_MODEL, D_MODEL), lambda l, j: (l, 0, j)),
            pl.BlockSpec((None, 1, D_MODEL), lambda l, j: (l, 0, j)),
        ],
        out_specs=pl.BlockSpec((None, rows, D_MODEL), lambda l, j: (l, 0, j)),
        out_shape=jax.ShapeDtypeStruct((depth, rows, 3 * D_MODEL), F32),
        name="modulation",
        compiler_params=_params(("arbitrary", "arbitrary")),
    )(c_all, w_mod, b_mod.reshape(depth, 1, 3 * D_MODEL))


def _inproj_kernel(x_ref, mod_ref, g_ref, w_ref, *refs, widths, widths_t):
    x = x_ref[...]
    y = x * lax.rsqrt(jnp.mean(x * x, axis=-1, keepdims=True) + EPS) * g_ref[...]
    mod = mod_ref[...]
    h = y * (1.0 + mod[:, D_MODEL:2 * D_MODEL]) + mod[:, :D_MODEL]
    hb = h.astype(BF16)
    out_refs = refs[1:] if widths_t else refs
    off = 0
    for o_ref, wd in zip(out_refs, widths):
        o_ref[...] = jnp.dot(hb, w_ref[:, off:off + wd], preferred_element_type=F32)
        off += wd
    if widths_t:
        wt_ref = refs[0]
        ht = h.T.astype(BF16)
        off = 0
        for o_ref, wd in zip(out_refs[len(widths):], widths_t):
            o_ref[...] = jnp.dot(wt_ref[off:off + wd, :], ht, preferred_element_type=F32)
            off += wd


def _inproj(x, mod, g, w, widths, rows_per_group, tm, wt=None, widths_t=()):
    m = x.shape[0]
    r = mod.shape[1]
    n = w.shape[1]
    groups = m // rows_per_group
    per_group = rows_per_group // tm
    in_specs = [
        pl.BlockSpec((tm, D_MODEL), lambda i: (i, 0)),
        pl.BlockSpec((None, r, 3 * D_MODEL), lambda i: ((i * tm) // rows_per_group, 0, 0)),
        pl.BlockSpec((1, D_MODEL), lambda i: (0, 0)),
        pl.BlockSpec((D_MODEL, n), lambda i: (0, 0), pipeline_mode=pl.Buffered(1)),
    ]
    args = [x, mod, g.reshape(1, D_MODEL), w]
    if widths_t:
        in_specs.append(pl.BlockSpec(wt.shape, lambda i: (0, 0), pipeline_mode=pl.Buffered(1)))
        args.append(wt)
    return pl.pallas_call(
        functools.partial(_inproj_kernel, widths=widths, widths_t=widths_t),
        grid=(m // tm,),
        in_specs=in_specs,
        out_specs=([pl.BlockSpec((tm, wd), lambda i: (i, 0)) for wd in widths]
                   + [pl.BlockSpec((None, wd, tm), lambda i: (i // per_group, 0, i % per_group)) for wd in widths_t]),
        out_shape=([jax.ShapeDtypeStruct((m, wd), F32) for wd in widths]
                   + [jax.ShapeDtypeStruct((groups, wd, rows_per_group), F32) for wd in widths_t]),
        name="inproj",
        compiler_params=_params(("arbitrary",), 56),
    )(*args)


def _outproj_kernel(a_ref, b_ref, x_ref, mod_ref, w_ref, fg_ref, o_ref, *, final):
    y = _dot(a_ref[...], w_ref[:HEAD_W, :]) + _dot(b_ref[...], w_ref[HEAD_W:, :])
    xn = x_ref[...] + mod_ref[...][:, 2 * D_MODEL:] * y
    if final:
        xn = xn * lax.rsqrt(jnp.mean(xn * xn, axis=-1, keepdims=True) + EPS) * fg_ref[...]
    o_ref[...] = xn


def _outproj(a, b, x, mod, w, final_g, rows_per_group, tm, final):
    m = x.shape[0]
    r = mod.shape[1]
    return pl.pallas_call(
        functools.partial(_outproj_kernel, final=final),
        grid=(m // tm,),
        in_specs=[
            pl.BlockSpec((tm, HEAD_W), lambda i: (i, 0)),
            pl.BlockSpec((tm, HEAD_W), lambda i: (i, 0)),
            pl.BlockSpec((tm, D_MODEL), lambda i: (i, 0)),
            pl.BlockSpec((None, r, 3 * D_MODEL), lambda i: ((i * tm) // rows_per_group, 0, 0)),
            pl.BlockSpec((2 * HEAD_W, D_MODEL), lambda i: (0, 0)),
            pl.BlockSpec((1, D_MODEL), lambda i: (0, 0)),
        ],
        out_specs=pl.BlockSpec((tm, D_MODEL), lambda i: (i, 0)),
        out_shape=jax.ShapeDtypeStruct((m, D_MODEL), F32),
        name="outproj",
        compiler_params=_params(("arbitrary",)),
    )(a, b, x, mod, w, final_g.reshape(1, D_MODEL))


def _later_matrix(n):
    return jnp.where(_iota((n, n), 0) > _iota((n, n), 1), 1.0, 0.0).astype(BF16)


def _sb_rows_tile(z2, valid, carry, later):
    sp = jnp.maximum(z2, 0.0) + jnp.log2(1.0 + jnp.exp2(-jnp.abs(z2)))
    if valid is not None:
        sp = jnp.where(valid, sp, 0.0)
    hi = sp.astype(BF16)
    lo = (sp - hi.astype(F32)).astype(BF16)
    t = sp + jnp.dot(hi, later, preferred_element_type=F32) + jnp.dot(lo, later, preferred_element_type=F32) + carry
    w = jnp.exp2(z2 - t)
    if valid is not None:
        w = jnp.where(valid, w, 0.0)
    return w, t[:, 0:1]


def _sb_prompt_kernel(q_ref, z_ref, kt_hbm, vt_hbm, o_ref, kt_sc, vt_sc, sem, acc_sc,
                      qst_sc, z_sc, sp_sc, hilo_sc, carry_sc, *, tq):
    b = pl.program_id(0)
    i = pl.program_id(1)

    @pl.when(i == 0)
    def _():
        ck = pltpu.make_async_copy(kt_hbm.at[b], kt_sc, sem.at[0])
        cv = pltpu.make_async_copy(vt_hbm.at[b], vt_sc, sem.at[1])
        ck.start()
        cv.start()
        ck.wait()
        cv.wait()

    n_pairs = SB_HEADS // 2
    rows_all = SB_HEADS * tq
    later = _later_matrix(tq)
    lane_head = _iota((tq, PAIR_W), 1) // SB_DIM
    row_head = _iota((PAIR_W, tq), 0) // SB_DIM
    q = q_ref[...] * (SB_DIM ** -0.5 * LOG2E)
    for p in range(n_pairs):
        qp = q[:, p * PAIR_W:(p + 1) * PAIR_W]
        for hh in range(2):
            qst_sc[p, hh * tq:(hh + 1) * tq, :] = jnp.where(lane_head == hh, qp, 0.0).astype(BF16)
    strictly_before = _iota((2 * tq, tq), 1) < _iota((2 * tq, tq), 0) % tq

    def tile(j, diag):
        start = pl.multiple_of(j * tq, tq)
        for p in range(n_pairs):
            rs = slice(p * 2 * tq, (p + 1) * 2 * tq)
            ktp = kt_sc[p * PAIR_W:(p + 1) * PAIR_W, pl.ds(start, tq)].astype(BF16)
            z2 = jnp.dot(qst_sc[p], ktp, preferred_element_type=F32)
            sp = jnp.maximum(z2, 0.0) + jnp.log2(1.0 + jnp.exp2(-jnp.abs(z2)))
            if diag:
                sp = jnp.where(strictly_before, sp, 0.0)
            hi = sp.astype(BF16)
            z_sc[rs, :] = z2
            sp_sc[rs, :] = sp
            hilo_sc[rs, :] = hi
            hilo_sc[p * 2 * tq + rows_all:(p + 1) * 2 * tq + rows_all, :] = (sp - hi.astype(F32)).astype(BF16)
        cum_hi = jnp.dot(hilo_sc[0:rows_all, :], later, preferred_element_type=F32)
        cum_lo = jnp.dot(hilo_sc[rows_all:2 * rows_all, :], later, preferred_element_type=F32)
        for p in range(n_pairs):
            rs = slice(p * 2 * tq, (p + 1) * 2 * tq)
            t = sp_sc[rs, :] + cum_hi[rs] + cum_lo[rs]
            if not diag:
                t = t + carry_sc[rs, :]
            w = jnp.exp2((z_sc[rs, :] - t).astype(BF16))
            if diag:
                w = jnp.where(strictly_before, w, jnp.zeros_like(w))
            carry_sc[rs, :] = t[:, 0:1]
            vtp = vt_sc[p * PAIR_W:(p + 1) * PAIR_W, pl.ds(start, tq)]
            w2 = jnp.concatenate([w[0:tq], w[tq:2 * tq]], axis=1)
            v2 = jnp.concatenate([jnp.where(row_head == hh, vtp, 0.0) for hh in range(2)], axis=1)
            upd = _dot_nt(w2, v2)
            if diag:
                acc_sc[:, p * PAIR_W:(p + 1) * PAIR_W] = upd
            else:
                acc_sc[:, p * PAIR_W:(p + 1) * PAIR_W] += upd
        return jnp.min(carry_sc[...])

    least = tile(i, True)

    def cond(c):
        return (c[0] >= 0) & (c[1] < SB_SKIP2)

    def body(c):
        return c[0] - 1, tile(c[0], False)

    lax.while_loop(cond, body, (i - 1, least))
    o_ref[...] = acc_sc[...] * _silu(z_ref[...])


def _sb_prompt(q, kt, vt, z, tq=128):
    b, s, _ = q.shape
    blk = pl.BlockSpec((None, tq, HEAD_W), lambda bi, i: (bi, i, 0))
    hbm = pl.BlockSpec(memory_space=pl.ANY)
    return pl.pallas_call(
        functools.partial(_sb_prompt_kernel, tq=tq),
        grid=(b, s // tq),
        in_specs=[blk, blk, hbm, hbm],
        out_specs=blk,
        out_shape=jax.ShapeDtypeStruct((b, s, HEAD_W), F32),
        scratch_shapes=[pltpu.VMEM((HEAD_W, s), F32),
                        pltpu.VMEM((HEAD_W, s), F32),
                        pltpu.SemaphoreType.DMA((2,)),
                        pltpu.VMEM((tq, HEAD_W), F32),
                        pltpu.VMEM((SB_HEADS // 2, 2 * tq, PAIR_W), BF16),
                        pltpu.VMEM((SB_HEADS * tq, tq), F32),
                        pltpu.VMEM((SB_HEADS * tq, tq), F32),
                        pltpu.VMEM((2 * SB_HEADS * tq, tq), BF16),
                        pltpu.VMEM((SB_HEADS * tq, 1), F32)],
        name="sb_prompt",
        compiler_params=_params(("arbitrary", "arbitrary"), 48),
    )(q, z, kt, vt)


def _tile_rows(x, reps):
    return jnp.concatenate([x] * reps, axis=0)


def _block_diag_queries(q, n_cols, head_dim):
    qt = _tile_rows(q, n_cols // SAMPLE_PAD)
    row_head = _iota((n_cols, HEAD_W), 0) // SAMPLE_PAD
    lane_head = _iota((n_cols, HEAD_W), 1) // head_dim
    return jnp.where(row_head == lane_head, qt, 0.0)


def _gather_heads(res, head_dim):
    lane_head = _iota((SAMPLE_PAD, HEAD_W), 1) // head_dim
    out = jnp.zeros((SAMPLE_PAD, HEAD_W), F32)
    for h in range(HEAD_W // head_dim):
        out = out + jnp.where(lane_head == h, res[h * SAMPLE_PAD:(h + 1) * SAMPLE_PAD, :], 0.0)
    return out


def _sb_sample_kernel(pt_ref, q_ref, kn_ref, vn_ref, z_ref, ck_hbm, cv_hbm, o_ref,
                      kbuf, vbuf, sem, acc_ref, kpad, vpad, *, n_pages, t_real):
    b = pl.program_id(0)
    nq = PAIR_W

    def copies(p, slot):
        phys = pt_ref[b * n_pages + p]
        return (pltpu.make_async_copy(ck_hbm.at[phys], kbuf.at[slot], sem.at[0, slot]),
                pltpu.make_async_copy(cv_hbm.at[phys], vbuf.at[slot], sem.at[1, slot]))

    def start(p, slot):
        for cp in copies(p, slot):
            cp.start()

    def wait(p, slot):
        for cp in copies(p, slot):
            cp.wait()

    start(n_pages - 1, (n_pages - 1) % 2)
    start(n_pages - 2, (n_pages - 2) % 2)

    @pl.when(b == 0)
    def _():
        kpad[...] = jnp.zeros_like(kpad)
        vpad[...] = jnp.zeros_like(vpad)

    kpad[0:SAMPLE_PAD, :] = kn_ref[...]
    vpad[0:SAMPLE_PAD, :] = vn_ref[...]

    qbd = _block_diag_queries(q_ref[...] * (SB_DIM ** -0.5 * LOG2E), nq, SB_DIM).astype(BF16)
    later = _later_matrix(PAGE)
    qrow = _iota((nq, PAGE), 0)
    key = _iota((nq, PAGE), 1)
    valid_new = (key < qrow % SAMPLE_PAD) & (key < t_real)
    row1 = _iota((nq, 1), 0)
    real_row = (row1 < SB_HEADS * SAMPLE_PAD) & (row1 % SAMPLE_PAD < t_real)

    w, carry = _sb_rows_tile(_dot_nt(qbd, kpad[...]), valid_new, jnp.zeros((nq, 1), F32), later)
    acc_ref[...] = _dot(w, vpad[...])

    def least(c):
        return jnp.min(jnp.where(real_row, c, jnp.inf))

    def cond(c):
        return (c[0] >= 0) & (c[2] < SB_SKIP2)

    def body(c):
        p, carry, _ = c
        slot = p % 2
        wait(p, slot)
        w, cn = _sb_rows_tile(jnp.dot(qbd, kbuf[slot].astype(BF16), preferred_element_type=F32), None, carry, later)
        acc_ref[...] += _dot_nt(w, vbuf[slot])

        @pl.when(p >= 2)
        def _():
            start(p - 2, slot)

        return p - 1, cn, least(cn)

    p_end, _, _ = lax.while_loop(cond, body, (jnp.int32(n_pages - 1), carry, least(carry)))

    @pl.when(p_end >= 0)
    def _():
        wait(p_end, p_end % 2)

    @pl.when(p_end >= 1)
    def _():
        wait(p_end - 1, (p_end - 1) % 2)

    o_ref[...] = _gather_heads(acc_ref[...], SB_DIM) * _silu(z_ref[...])


def _sb_sample(q, kn, vn, z, cache_k, cache_v, page_table, t_real):
    nb = q.shape[0]
    n_pages = page_table.shape[1]
    blk = pl.BlockSpec((None, SAMPLE_PAD, HEAD_W), lambda bi, pt: (bi, 0, 0))
    hbm = pl.BlockSpec(memory_space=pl.ANY)
    return pl.pallas_call(
        functools.partial(_sb_sample_kernel, n_pages=n_pages, t_real=t_real),
        grid_spec=pltpu.PrefetchScalarGridSpec(
            num_scalar_prefetch=1,
            grid=(nb,),
            in_specs=[blk, blk, blk, blk, hbm, hbm],
            out_specs=blk,
            scratch_shapes=[
                pltpu.VMEM((2, HEAD_W, PAGE), F32),
                pltpu.VMEM((2, HEAD_W, PAGE), F32),
                pltpu.SemaphoreType.DMA((2, 2)),
                pltpu.VMEM((PAIR_W, HEAD_W), F32),
                pltpu.VMEM((PAGE, HEAD_W), F32),
                pltpu.VMEM((PAGE, HEAD_W), F32),
            ],
        ),
        out_shape=jax.ShapeDtypeStruct((nb, SAMPLE_PAD, HEAD_W), F32),
        name="sb_sample",
        compiler_params=_params(("arbitrary",)),
    )(page_table.reshape(-1), q, kn, vn, z, cache_k, cache_v)


def _mlstm_kernel(q_ref, k_ref, v_ref, og_ref, z_ref, gate_ref, bias_ref, gn_ref, c0_ref, n0_ref, m0_ref,
                  out_ref, c_out, n_out, m_out, c_sc, n_sc, m_sc, *, chunk, valid, per_step):
    ci = pl.program_id(1)

    @pl.when(ci == 0)
    def _():
        c_sc[...] = c0_ref[...]
        n_sc[...] = n0_ref[...]
        m_sc[...] = m0_ref[...]

    for bb in range(per_step):
        _mlstm_chunk(q_ref.at[bb], k_ref.at[bb], v_ref.at[bb], og_ref.at[bb], z_ref.at[bb], gate_ref.at[bb],
                     bias_ref, gn_ref, out_ref.at[bb], c_sc.at[bb], n_sc.at[bb], m_sc.at[bb], chunk, valid)

    @pl.when(ci == pl.num_programs(1) - 1)
    def _():
        c_out[...] = c_sc[...]
        n_out[...] = n_sc[...]
        m_out[...] = m_sc[...]


def _mlstm_chunk(q_ref, k_ref, v_ref, og_ref, z_ref, gate_ref, bias_ref, gn_ref, out_ref, c_sc, n_sc, m_sc,
                 chunk, valid):
    pre = gate_ref[...] + bias_ref[...]
    lf = pltpu.roll(_log_sigmoid(pre), PAIR_W - ML_HEADS, 1)
    ig = pre
    if valid < chunk:
        live = _iota((chunk, PAIR_W), 0) < valid
        lf = jnp.where(live, lf, 0.0)
        ig = jnp.where(live, ig, -jnp.inf)
    r = _iota((chunk, chunk), 0)
    c = _iota((chunk, chunk), 1)
    tril = c <= r
    lf_hi = lf.astype(BF16)
    bc2 = jnp.dot(jnp.where(tril, 1.0, 0.0).astype(BF16),
                  jnp.concatenate([lf_hi, (lf - lf_hi.astype(F32)).astype(BF16)], axis=1), preferred_element_type=F32)
    bc = bc2[:, :PAIR_W] + bc2[:, PAIR_W:]
    a = ig - bc
    a_t = a.T
    q_all = q_ref[...]
    k_all = k_ref[...] * (ML_DIM ** -0.5)
    v_all = v_ref[...]
    rows_st = ML_HEADS * chunk
    q_st = jnp.where(_iota((rows_st, HEAD_W), 0) // chunk == _iota((rows_st, HEAD_W), 1) // ML_DIM,
                     jnp.concatenate([q_all] * ML_HEADS, axis=0), 0.0).astype(BF16)
    qk_st = _dot_nt(q_st, k_all)
    qc_st = _dot(q_st, c_sc[...].reshape(ML_HEADS * ML_DIM, ML_DIM))
    wts, ms, inters, wks, decays = [], [], [], [], []
    for h in range(ML_HEADS):
        sl = slice(h * ML_DIM, (h + 1) * ML_DIM)
        bc_h = bc[:, h:h + 1]
        a_row = a_t[h:h + 1, :]
        m0 = m_sc[h:h + 1, 0:1]
        cm = jnp.max(jnp.where(tril, a_row, -jnp.inf), axis=-1, keepdims=True)
        m_h = jnp.maximum(bc_h + m0, cm + bc_h)
        dmat = jnp.exp(jnp.where(tril, bc_h - m_h + a_row, -jnp.inf))
        wts.append(dmat * qk_st[h * chunk:(h + 1) * chunk])
        ms.append(m_h)
        inters.append(jnp.exp(bc_h + m0 - m_h))
        bc_l = bc_h[chunk - 1:chunk, :]
        m_l = m_h[chunk - 1:chunk, :]
        decays.append(jnp.exp(bc_l + m0 - m_l))
        wks.append(jnp.exp(bc_l + a[:, h:h + 1] - m_l) * k_all[:, sl])
        m_sc[h:h + 1, :] = jnp.broadcast_to(m_l, (1, PAIR_W))
    wv_st = _dot(jnp.concatenate(wts, axis=0), v_all)
    wk_all = jnp.concatenate(wks, axis=1)
    upd = _dot_tn(wk_all, v_all)
    for h in range(ML_HEADS):
        sl = slice(h * ML_DIM, (h + 1) * ML_DIM)
        rs = slice(h * chunk, (h + 1) * chunk)
        n_h = n_sc[h:h + 1, :]
        num = inters[h] * qc_st[rs] + wv_st[rs, sl]
        den = (inters[h] * jnp.sum(q_all[:, sl] * n_h, axis=-1, keepdims=True)
               + jnp.sum(wts[h], axis=-1, keepdims=True))
        hv = num / jnp.maximum(jnp.abs(den), jnp.exp(-ms[h]))
        hn = hv * lax.rsqrt(jnp.mean(hv * hv, axis=-1, keepdims=True) + EPS) * gn_ref[:, sl]
        out_ref[:, sl] = hn * _sigmoid(og_ref[:, sl]) * _silu(z_ref[:, sl])
        c_sc[h] = decays[h] * c_sc[h] + upd[sl, sl]
        n_sc[h:h + 1, :] = decays[h] * n_h + jnp.sum(wks[h], axis=0, keepdims=True)


ML_PER_STEP = 1


def _mlstm(q, k, v, og, z, gates, bias, gn, c0, n0, m0, nb, chunk, valid):
    m = q.shape[0]
    t = m // nb
    n_chunks = t // chunk
    per = ML_PER_STEP
    tok = pl.BlockSpec((per, chunk, HEAD_W), lambda g, ci: (g, ci, 0))
    st_c = pl.BlockSpec((per, ML_HEADS, ML_DIM, ML_DIM), lambda g, ci: (g, 0, 0, 0))
    st_v = pl.BlockSpec((per, SAMPLE_PAD, ML_DIM), lambda g, ci: (g, 0, 0))
    r3 = lambda u: u.reshape(nb, t, u.shape[-1])
    out, c1, n1, m1 = pl.pallas_call(
        functools.partial(_mlstm_kernel, chunk=chunk, valid=valid, per_step=per),
        grid=(nb // per, n_chunks),
        in_specs=[tok, tok, tok, tok, tok,
                  pl.BlockSpec((per, chunk, PAIR_W), lambda g, ci: (g, ci, 0)),
                  pl.BlockSpec((1, PAIR_W), lambda g, ci: (0, 0)),
                  pl.BlockSpec((1, HEAD_W), lambda g, ci: (0, 0)),
                  st_c, st_v, st_v],
        out_specs=[tok, st_c, st_v, st_v],
        out_shape=[jax.ShapeDtypeStruct((nb, t, HEAD_W), F32),
                   jax.ShapeDtypeStruct((nb, ML_HEADS, ML_DIM, ML_DIM), F32),
                   jax.ShapeDtypeStruct((nb, SAMPLE_PAD, ML_DIM), F32),
                   jax.ShapeDtypeStruct((nb, SAMPLE_PAD, ML_DIM), F32)],
        scratch_shapes=[pltpu.VMEM((per, ML_HEADS, ML_DIM, ML_DIM), F32),
                        pltpu.VMEM((per, SAMPLE_PAD, ML_DIM), F32),
                        pltpu.VMEM((per, SAMPLE_PAD, ML_DIM), F32)],
        name="mlstm",
        compiler_params=_params(("arbitrary", "arbitrary")),
    )(r3(q), r3(k), r3(v), r3(og), r3(z), r3(gates), bias, gn, c0, n0, m0)
    return out.reshape(m, HEAD_W), c1, n1, m1


def _gla_kernel(q_ref, k_ref, v_ref, z_ref, gg_ref, wg_ref, bg_ref, gn_ref, s0_ref, out_ref, s_out,
                st_sc, kb, gb, vb, qa_sc, kc_sc, dec_sc, oi_sc, *, tb, sub, valid):
    ti = pl.program_id(1)
    kw = GLA_HEADS * GLA_DK

    @pl.when(ti == 0)
    def _():
        for p in range(2):
            st_sc[p] = s0_ref[p].T
        kb[0:sub, :] = jnp.zeros((sub, kw), F32)
        gb[0:sub, :] = jnp.zeros((sub, kw), F32)
        vb[0:sub, :] = jnp.zeros((sub, HEAD_W), F32)

    lg = _log_sigmoid(_dot(gg_ref[...], wg_ref[...]) + bg_ref[...]) * (1.0 / GLA_TAU)
    k = k_ref[...]
    if valid < tb:
        live = _iota((tb, kw), 0) < valid
        lg = jnp.where(live, lg, 0.0)
        k = jnp.where(live, k, 0.0)
    r = _iota((tb, tb), 0)
    c = _iota((tb, tb), 1)
    same = (r // sub) == (c // sub)
    sums = jnp.concatenate([jnp.where(same & (c <= r), 1.0, 0.0), jnp.where(same, 1.0, 0.0)], axis=0).astype(BF16)
    lg_hi = lg.astype(BF16)
    parts = jnp.dot(sums, jnp.concatenate([lg_hi, (lg - lg_hi.astype(F32)).astype(BF16)], axis=1),
                    preferred_element_type=F32)
    g = parts[:tb, :kw] + parts[:tb, kw:]
    gend = parts[tb:, :kw] + parts[tb:, kw:]
    q = q_ref[...] * (GLA_DK ** -0.5)
    v = v_ref[...]
    qa_sc[...] = q * jnp.exp(g)
    kc_sc[...] = k * jnp.exp(gend - g)
    dec_sc[...] = jnp.exp(gend)
    kb[sub:sub + tb, :] = k
    gb[sub:sub + tb, :] = g
    vb[sub:sub + tb, :] = v

    head_spread = jnp.where(_iota((kw, HEAD_W), 0) // GLA_DK == _iota((kw, HEAD_W), 1) // GLA_DV, 1.0, 0.0).astype(BF16)
    rmod = _iota((tb, 1), 0) % sub
    oi = jnp.zeros((tb, HEAD_W), F32)
    for d in range(sub):
        lo = sub - d
        dec = jnp.exp(jnp.where(rmod >= d, g - gb[lo:lo + tb, :], NEG))
        oi = oi + _dot(q * kb[lo:lo + tb, :] * dec, head_spread) * vb[lo:lo + tb, :]
    oi_sc[...] = oi

    lane_head = _iota((2 * sub, PAIR_W), 1) // GLA_DK
    row_head = _iota((2 * sub, PAIR_W), 0) // sub
    for i in range(tb // sub):
        rows = slice(i * sub, (i + 1) * sub)
        qa = qa_sc[rows, :]
        kc = kc_sc[rows, :]
        dec = dec_sc[rows, :][0:1, :]
        for p in range(2):
            ps = slice(p * PAIR_W, (p + 1) * PAIR_W)
            vs = slice(2 * p * GLA_DV, (2 * p + 2) * GLA_DV)
            st = st_sc[p]
            own = lane_head == row_head
            o2 = _dot_nt(jnp.where(own, jnp.concatenate([qa[:, ps]] * 2, axis=0), 0.0), st)
            v2 = jnp.concatenate([v_ref[rows, vs][:, :GLA_DV], v_ref[rows, vs][:, GLA_DV:]], axis=0)
            st_sc[p] = st * dec[:, ps] + _dot_tn(v2, jnp.where(own, jnp.concatenate([kc[:, ps]] * 2, axis=0), 0.0))
            for hh in range(2):
                hs = slice((2 * p + hh) * GLA_DV, (2 * p + hh + 1) * GLA_DV)
                o = o2[hh * sub:(hh + 1) * sub] + oi_sc[rows, hs]
                on = o * lax.rsqrt(jnp.mean(o * o, axis=-1, keepdims=True) + EPS) * gn_ref[:, hs]
                out_ref[rows, hs] = on * _silu(z_ref[rows, hs])

    @pl.when(ti == pl.num_programs(1) - 1)
    def _():
        for p in range(2):
            s_out[p] = st_sc[p].T


def _gla(q, k, v, z, gg, wg, bg, gn, s0, nb, tb, sub, valid):
    m = q.shape[0]
    nt = m // (nb * tb)
    kw = GLA_HEADS * GLA_DK
    tok_k = pl.BlockSpec((tb, kw), lambda b, t: (b * nt + t, 0))
    tok_v = pl.BlockSpec((tb, HEAD_W), lambda b, t: (b * nt + t, 0))
    st = pl.BlockSpec((None, 2, PAIR_W, GLA_DV), lambda b, t: (b, 0, 0, 0))
    return pl.pallas_call(
        functools.partial(_gla_kernel, tb=tb, sub=sub, valid=valid),
        grid=(nb, nt),
        in_specs=[tok_k, tok_k, tok_v, tok_v,
                  pl.BlockSpec((tb, PAIR_W), lambda b, t: (b * nt + t, 0)),
                  pl.BlockSpec((PAIR_W, kw), lambda b, t: (0, 0)),
                  pl.BlockSpec((1, kw), lambda b, t: (0, 0)),
                  pl.BlockSpec((1, HEAD_W), lambda b, t: (0, 0)),
                  st],
        out_specs=[tok_v, st],
        out_shape=[jax.ShapeDtypeStruct((m, HEAD_W), F32),
                   jax.ShapeDtypeStruct((nb, 2, PAIR_W, GLA_DV), F32)],
        scratch_shapes=[pltpu.VMEM((2, GLA_DV, PAIR_W), F32),
                        pltpu.VMEM((sub + tb, kw), F32),
                        pltpu.VMEM((sub + tb, kw), F32),
                        pltpu.VMEM((sub + tb, HEAD_W), F32),
                        pltpu.VMEM((tb, kw), F32),
                        pltpu.VMEM((tb, kw), F32),
                        pltpu.VMEM((tb, kw), F32),
                        pltpu.VMEM((tb, HEAD_W), F32)],
        name="gla",
        compiler_params=_params(("arbitrary", "arbitrary")),
    )(q, k, v, z, gg, wg, bg, gn, s0)


def _top3_rows(s, n_valid):
    nb = s.shape[0]
    nrow = _iota(s.shape, 0)
    live = nrow < n_valid
    s = jnp.where(live, s, -jnp.inf)
    chosen = jnp.zeros(s.shape, F32)
    picks = []
    for _ in range(MB_TOPK):
        mx = jnp.max(s, axis=0, keepdims=True)
        idx = jnp.min(jnp.where(s == mx, nrow, nb), axis=0, keepdims=True)
        pick = nrow == idx
        chosen = jnp.where(pick, jnp.where(live, 1.0, 0.0), chosen)
        s = jnp.where(pick, -jnp.inf, s)
        picks.append(idx)
    return picks, chosen


KEY_CHUNK = 64
SUM_ROWS = 16


def _moba_prompt_kernel(q_ref, kt_ref, vt_ref, z_ref, o_ref,
                        ktok_sc, vtbd_sc, km_sc, bias_sc, sel2_sc, sel_sc, qtm_sc, sa_sc, sb_sc, sd_sc,
                        cma_sc, cmb_sc, cmd_sc, p_sc, acc_sc, m_sc, *, tq, nblk):
    p = pl.program_id(1)
    i = pl.program_id(2)
    slope2 = [jnp.exp2(-(jnp.full((1, tq), 2 * p + hh + 1, jnp.int32)).astype(F32)) * LOG2E for hh in range(2)]

    row_head = _iota((PAIR_W, tq), 0) // MB_DIM

    @pl.when(i == 0)
    def _():
        lane = _iota((PAIR_W, PAIR_W), 1)
        feat_head = _iota((PAIR_W, MB_BLOCK), 0) // MB_DIM
        km = jnp.zeros((PAIR_W, PAIR_W), F32)
        for n in range(nblk):
            cols = slice(n * MB_BLOCK, (n + 1) * MB_BLOCK)
            blk = kt_ref[:, cols]
            ktok_sc[cols, :] = blk.T.astype(BF16)
            km = jnp.where(lane == n, jnp.mean(blk, axis=1, keepdims=True), km)
            vblk = vt_ref[:, cols]
            ones_row = _iota((SUM_ROWS, MB_BLOCK), 0)
            for hh in range(2):
                vtbd_sc[0:PAIR_W, (2 * n + hh) * MB_BLOCK:(2 * n + hh + 1) * MB_BLOCK] = (
                    jnp.where(feat_head == hh, vblk, 0.0).astype(BF16))
                vtbd_sc[PAIR_W:PAIR_W + SUM_ROWS, (2 * n + hh) * MB_BLOCK:(2 * n + hh + 1) * MB_BLOCK] = (
                    jnp.where(ones_row == hh, 1.0, 0.0).astype(BF16))
        km_t = km.T
        blk_head = _iota((PAIR_W, PAIR_W), 1) // MB_DIM
        for hh in range(2):
            km_sc[hh * nblk:(hh + 1) * nblk, :] = jnp.where(blk_head == hh, km_t, 0.0)[0:nblk]
        key_off = _iota((MB_BLOCK, tq), 0).astype(F32)
        for hh in range(2):
            bias_sc[hh] = slope2[hh] * key_off

    q_t = q_ref[...].T
    s_sel = jnp.dot(km_sc[...], q_t, precision=lax.Precision.HIGHEST, preferred_element_type=F32)
    for hh in range(2):
        _, chosen = _top3_rows(s_sel[hh * nblk:(hh + 1) * nblk], i)
        sel2_sc[...] = chosen
        for n in range(nblk):
            sel_sc[hh, n] = sel2_sc[n:n + 1, :]
        qtm_sc[:, hh * tq:(hh + 1) * tq] = jnp.where(row_head == hh, q_t * (MB_DIM ** -0.5 * LOG2E), 0.0).astype(BF16)

    n_chunks = MB_BLOCK // KEY_CHUNK

    bufs = ((sa_sc, cma_sc), (sb_sc, cmb_sc), (sd_sc, cmd_sc))
    acc_row = _iota((PAIR_W + SUM_ROWS, tq), 0)
    acc_head0 = (acc_row < MB_DIM) | (acc_row == PAIR_W)

    def scores(n, buf, diag):
        s_ref, cm_ref = bufs[buf]
        start = pl.multiple_of(n * MB_BLOCK, MB_BLOCK)
        s = jnp.dot(ktok_sc[pl.ds(start, MB_BLOCK), :], qtm_sc[...], preferred_element_type=F32)
        for hh in range(2):
            cm = None
            for kc in range(n_chunks):
                rows = slice(kc * KEY_CHUNK, (kc + 1) * KEY_CHUNK)
                sc = s[rows, hh * tq:(hh + 1) * tq] + bias_sc[hh, rows]
                if diag:
                    key = _iota((KEY_CHUNK, tq), 0) + kc * KEY_CHUNK
                    sc = jnp.where(key <= _iota((KEY_CHUNK, tq), 1), sc, NEG)
                s_ref[hh, rows] = sc
                c = jnp.max(sc, axis=0, keepdims=True)
                cm = c if cm is None else jnp.maximum(cm, c)
            cm_ref[hh] = cm

    def attend(n, buf, diag):
        s_ref, cm_ref = bufs[buf]
        alphas = []
        for hh in range(2):
            cm = cm_ref[hh]
            if diag:
                m_new = cm
                shift = cm
            else:
                c = slope2[hh] * ((n - i) * MB_BLOCK).astype(F32)
                ok = sel_sc[hh, n] > 0.5
                m_old = m_sc[hh]
                m_new = jnp.maximum(m_old, jnp.where(ok, cm + c, NEG))
                shift = jnp.where(ok, m_new - c, -NEG)
                alphas.append(jnp.exp2(m_old - m_new))
            for kc in range(n_chunks):
                rows = slice(kc * KEY_CHUNK, (kc + 1) * KEY_CHUNK)
                p_sc[hh * MB_BLOCK + kc * KEY_CHUNK:hh * MB_BLOCK + (kc + 1) * KEY_CHUNK, :] = (
                    jnp.exp2((s_ref[hh, rows] - shift).astype(BF16)))
            m_sc[hh] = m_new
        vt = vtbd_sc[:, pl.ds(pl.multiple_of(n * (2 * MB_BLOCK), 2 * MB_BLOCK), 2 * MB_BLOCK)]
        pv = jnp.dot(vt, p_sc[...], preferred_element_type=F32)
        if diag:
            acc_sc[...] = pv
        else:
            acc_sc[...] = acc_sc[...] * jnp.where(acc_head0, alphas[0], alphas[1]) + pv

    scores(i, 2, True)
    scores(0, 0, False)
    scores(1, 1, False)
    attend(i, 2, True)

    def body(k, _):
        n0 = 2 * k
        attend(n0, 0, False)
        scores(jnp.minimum(n0 + 2, nblk - 1), 0, False)
        attend(n0 + 1, 1, False)
        scores(jnp.minimum(n0 + 3, nblk - 1), 1, False)
        return 0

    lax.fori_loop(0, (i + 1) // 2, body, 0)
    inv = [1.0 / acc_sc[PAIR_W + hh:PAIR_W + hh + 1, :] for hh in range(2)]
    out_t = acc_sc[0:PAIR_W, :] * jnp.where(row_head == 0, inv[0], inv[1])
    o_ref[...] = out_t.T * _silu(z_ref[...])


def _moba_prompt(q, kt, vt, z):
    b, s, _ = q.shape
    tq = MB_BLOCK
    nblk = s // MB_BLOCK
    assert nblk <= PAIR_W
    blk_q = pl.BlockSpec((None, tq, PAIR_W), lambda bi, p, i: (bi, i, p))
    blk_kv = pl.BlockSpec((None, PAIR_W, s), lambda bi, p, i: (bi, p, 0))
    return pl.pallas_call(
        functools.partial(_moba_prompt_kernel, tq=tq, nblk=nblk),
        grid=(b, HEAD_W // PAIR_W, s // tq),
        in_specs=[blk_q, blk_kv, blk_kv, blk_q],
        out_specs=blk_q,
        out_shape=jax.ShapeDtypeStruct((b, s, HEAD_W), F32),
        scratch_shapes=[pltpu.VMEM((s, PAIR_W), BF16),
                        pltpu.VMEM((PAIR_W + SUM_ROWS, 2 * s), BF16),
                        pltpu.VMEM((2 * nblk, PAIR_W), F32),
                        pltpu.VMEM((2, MB_BLOCK, tq), F32),
                        pltpu.VMEM((nblk, tq), F32),
                        pltpu.VMEM((2, nblk, 1, tq), F32),
                        pltpu.VMEM((PAIR_W, 2 * tq), BF16),
                        pltpu.VMEM((2, MB_BLOCK, tq), F32),
                        pltpu.VMEM((2, MB_BLOCK, tq), F32),
                        pltpu.VMEM((2, MB_BLOCK, tq), F32),
                        pltpu.VMEM((2, 1, tq), F32),
                        pltpu.VMEM((2, 1, tq), F32),
                        pltpu.VMEM((2, 1, tq), F32),
                        pltpu.VMEM((2 * MB_BLOCK, tq), BF16),
                        pltpu.VMEM((PAIR_W + SUM_ROWS, tq), F32),
                        pltpu.VMEM((2, 1, tq), F32)],
        name="moba_prompt",
        compiler_params=_params(("arbitrary", "arbitrary", "arbitrary")),
    )(q, kt, vt, z)


_MEAN_PAGES = 32


def _top3_lanes(s, n_valid):
    width = s.shape[1]
    lane = _iota(s.shape, 1)
    s = jnp.where(lane < n_valid, s, -jnp.inf)
    picks = []
    for _ in range(MB_TOPK):
        mx = jnp.max(s, axis=1, keepdims=True)
        idx = jnp.min(jnp.where(s == mx, lane, width), axis=1, keepdims=True)
        s = jnp.where(lane == idx, -jnp.inf, s)
        picks.append(idx)
    return picks


def _mb_select_kernel(pt_ref, *refs, n_blocks):
    pages = refs[:_MEAN_PAGES]
    q_ref, idx_ref, km_sc = refs[_MEAN_PAGES:]
    s = pl.program_id(1)
    per_blk = MB_BLOCK // PAGE
    blk_per_step = _MEAN_PAGES // per_blk

    @pl.when(s == 0)
    def _():
        km_sc[...] = jnp.zeros_like(km_sc)

    lane = _iota((HEAD_W, PAIR_W), 1)
    km = km_sc[...]
    for j in range(blk_per_step):
        both = sum(pages[per_blk * j + e][...] for e in range(per_blk))
        km = jnp.where(lane == s * blk_per_step + j, jnp.sum(both, axis=1, keepdims=True) * (1.0 / MB_BLOCK), km)
    km_sc[...] = km

    @pl.when(s == pl.num_programs(1) - 1)
    def _():
        qbd = _block_diag_queries(q_ref[...], PAIR_W, MB_DIM)
        sc = jnp.dot(qbd, km, precision=lax.Precision.HIGHEST, preferred_element_type=F32)
        picks = _top3_lanes(sc, n_blocks)
        lane_o = _iota((PAIR_W, PAIR_W), 1)
        out = jnp.zeros((PAIR_W, PAIR_W), jnp.int32)
        for k, pk in enumerate(picks):
            out = jnp.where(lane_o == k, pk, out)
        idx_ref[...] = out


def _mb_select(q, cache_k, page_table):
    nb = q.shape[0]
    n_pages = page_table.shape[1]
    steps = n_pages // _MEAN_PAGES
    n_blocks = n_pages * PAGE // MB_BLOCK
    assert n_pages % _MEAN_PAGES == 0 and MB_TOPK <= n_blocks <= PAIR_W
    page_specs = [pl.BlockSpec((None, HEAD_W, PAGE), lambda bi, s, pt, j=j: (pt[bi * n_pages + s * _MEAN_PAGES + j], 0, 0))
                  for j in range(_MEAN_PAGES)]
    return pl.pallas_call(
        functools.partial(_mb_select_kernel, n_blocks=n_blocks),
        grid_spec=pltpu.PrefetchScalarGridSpec(
            num_scalar_prefetch=1,
            grid=(nb, steps),
            in_specs=page_specs + [pl.BlockSpec((None, SAMPLE_PAD, HEAD_W), lambda bi, s, pt: (bi, 0, 0))],
            out_specs=pl.BlockSpec((None, PAIR_W, PAIR_W), lambda bi, s, pt: (bi, 0, 0)),
            scratch_shapes=[pltpu.VMEM((HEAD_W, PAIR_W), F32)],
        ),
        out_shape=jax.ShapeDtypeStruct((nb, PAIR_W, PAIR_W), jnp.int32),
        name="moba_select",
        compiler_params=_params(("arbitrary", "arbitrary")),
    )(page_table.reshape(-1), *([cache_k] * _MEAN_PAGES), q)


def _mb_sample_kernel(pt_ref, sel_ref, q_ref, kn_ref, vn_ref, z_ref, idx_ref, ck_hbm, cv_hbm, o_ref,
                      kbuf, vbuf, sem, kpad, vpad, s_sc, *, n_pages, t_real):
    b = pl.program_id(0)
    per_blk = MB_BLOCK // PAGE
    n_grp = MB_TOPK * t_real
    n_sel = n_grp * MB_BLOCK
    past = n_pages * PAGE

    def copies(req, slot, h, k, t, j):
        blk = sel_ref[(req * MB_HEADS * t_real + h * t_real + t) * MB_TOPK + k]
        phys = pt_ref[req * n_pages + blk * per_blk + j]
        hs = pl.ds(h * MB_DIM, MB_DIM)
        ls = pl.ds(((k * t_real + t) * per_blk + j) * PAGE, PAGE)
        return (pltpu.make_async_copy(ck_hbm.at[phys, hs, :], kbuf.at[slot, hs, ls], sem.at[0, slot]),
                pltpu.make_async_copy(cv_hbm.at[phys, hs, :], vbuf.at[slot, hs, ls], sem.at[1, slot]))

    todo = [(h, k, t, j) for h in range(MB_HEADS) for k in range(MB_TOPK) for t in range(t_real)
            for j in range(per_blk)]

    def fetch(req, slot):
        for a in todo:
            for cp in copies(req, slot, *a):
                cp.start()

    slot = b % 2

    @pl.when(b == 0)
    def _():
        fetch(0, 0)
        kpad[...] = jnp.zeros_like(kpad)
        vpad[...] = jnp.zeros_like(vpad)

    @pl.when(b + 1 < pl.num_programs(0))
    def _():
        fetch(b + 1, 1 - slot)

    kpad[0:SAMPLE_PAD, :] = kn_ref[...]
    vpad[0:SAMPLE_PAD, :] = vn_ref[...]

    nq = PAIR_W
    qbd = _block_diag_queries(q_ref[...] * (MB_DIM ** -0.5), nq, MB_DIM).astype(BF16)
    row = _iota((nq, 1), 0)
    t_row = row % SAMPLE_PAD
    slope = jnp.exp2(-(row // SAMPLE_PAD + 1).astype(F32))

    key = _iota((nq, PAGE), 1)
    s_new = _dot_nt(qbd, kpad[...]) - slope * (t_row - key).astype(F32)
    s_new = jnp.where((key <= t_row) & (key < t_real), s_new, NEG)
    s_sc[:, n_sel:n_sel + PAGE] = s_new
    m = jnp.max(s_new, axis=1, keepdims=True)

    for a in todo:
        for cp in copies(b, slot, *a):
            cp.wait()

    key_in_blk = _iota((nq, MB_BLOCK), 1)
    for g in range(n_grp):
        k, t = divmod(g, t_real)
        lanes = slice(g * MB_BLOCK, (g + 1) * MB_BLOCK)
        pos = idx_ref[:, k:k + 1] * MB_BLOCK + key_in_blk
        sg = _dot(qbd, kbuf[slot, :, lanes]) - slope * (past + t_row - pos).astype(F32)
        sg = jnp.where(t_row == t, sg, NEG)
        s_sc[:, lanes] = sg
        m = jnp.maximum(m, jnp.max(sg, axis=1, keepdims=True))

    pr = jnp.exp(s_sc[...] - m)
    num = _dot_nt(pr[:, :n_sel], vbuf[slot]) + _dot(pr[:, n_sel:], vpad[...])
    res = num / jnp.sum(pr, axis=1, keepdims=True)
    o_ref[...] = _gather_heads(res, MB_DIM) * _silu(z_ref[...])


def _mb_sample(q, kn, vn, z, idx, cache_k, cache_v, page_table, t_real):
    nb = q.shape[0]
    n_pages = page_table.shape[1]
    n_sel = MB_TOPK * t_real * MB_BLOCK
    sel = idx[:, :MB_HEADS * SAMPLE_PAD, :MB_TOPK].reshape(nb, MB_HEADS, SAMPLE_PAD, MB_TOPK)[:, :, :t_real]
    blk = pl.BlockSpec((None, SAMPLE_PAD, HEAD_W), lambda bi, pt, sl: (bi, 0, 0))
    hbm = pl.BlockSpec(memory_space=pl.ANY)
    return pl.pallas_call(
        functools.partial(_mb_sample_kernel, n_pages=n_pages, t_real=t_real),
        grid_spec=pltpu.PrefetchScalarGridSpec(
            num_scalar_prefetch=2,
            grid=(nb,),
            in_specs=[blk, blk, blk, blk,
                      pl.BlockSpec((None, PAIR_W, PAIR_W), lambda bi, pt, sl: (bi, 0, 0)),
                      hbm, hbm],
            out_specs=blk,
            scratch_shapes=[pltpu.VMEM((2, HEAD_W, n_sel), F32),
                            pltpu.VMEM((2, HEAD_W, n_sel), F32),
                            pltpu.SemaphoreType.DMA((2, 2)),
                            pltpu.VMEM((PAGE, HEAD_W), F32),
                            pltpu.VMEM((PAGE, HEAD_W), F32),
                            pltpu.VMEM((PAIR_W, n_sel + PAGE), F32)],
        ),
        out_shape=jax.ShapeDtypeStruct((nb, SAMPLE_PAD, HEAD_W), F32),
        name="moba_sample",
        compiler_params=_params(("arbitrary",), 48),
    )(page_table.reshape(-1), sel.reshape(-1), q, kn, vn, z, idx, cache_k, cache_v)


def _pad_cols(w, width):
    return jnp.pad(w, ((0, 0), (0, width - w.shape[1])))


def _pages_feature_major(cache):
    n_pool, page, heads, dim = cache.shape
    return jnp.transpose(cache, (0, 2, 3, 1)).reshape(n_pool, heads * dim, page)


def kernel(x_prompt, x_sample, cache_sb_k, cache_sb_v, cache_mb_k, cache_mb_v, state_ml_C, state_ml_n, state_ml_m, state_gla_S, page_table, c_prompt, c_sample, w_mod, b_mod, norm_g, final_g, w_in_even, w_out_even, ml_b_i, ml_b_f, ml_gn, w_in_odd, w_out_odd, gla_w_g2, gla_b_g, gla_gn):
    bsz, seq, _ = x_prompt.shape
    db, t_real, _ = x_sample.shape
    mp = bsz * seq
    ms = db * SAMPLE_PAD

    rows = -(-(bsz + db) // 8) * 8
    c_all = jnp.pad(jnp.concatenate([c_prompt, c_sample], axis=0), ((0, rows - bsz - db), (0, 0)))
    mod = _modulation(c_all, w_mod, b_mod)
    mod_p = mod[:, :bsz, None, :]
    mod_s = jnp.repeat(mod[:, bsz:bsz + db], SAMPLE_PAD, axis=1)[:, None]

    xp = x_prompt.reshape(mp, D_MODEL)
    xs = jnp.pad(x_sample, ((0, 0), (0, SAMPLE_PAD - t_real), (0, 0))).reshape(ms, D_MODEL)
    tm = 512

    w_even = jnp.concatenate([w_in_even[0, :, :9 * HEAD_W], _pad_cols(w_in_even[0, :, 9 * HEAD_W:], PAIR_W)],
                             axis=1).astype(BF16)
    widths_even = (HEAD_W,) * 9 + (PAIR_W,)
    gate_bias = _pad_cols(jnp.concatenate([ml_b_i[0], ml_b_f[0]])[None, :], PAIR_W)
    ml_gn0 = ml_gn[0].reshape(1, HEAD_W)
    w_out0 = w_out_even[0].astype(BF16)

    w_even_tok = jnp.concatenate([w_even[:, :HEAD_W], w_even[:, 3 * HEAD_W:]], axis=1)
    w_even_kv = w_even[:, HEAD_W:3 * HEAD_W].T
    sq, sz, mq, mk, mv, mo, mz, mg, sk, sv = _inproj(xp, mod_p[0], norm_g[0], w_even_tok, (HEAD_W,) * 7 + (PAIR_W,),
                                                     seq, tm, w_even_kv, (HEAD_W, HEAD_W))
    rp = lambda u: u.reshape(bsz, seq, HEAD_W)
    a_p = _sb_prompt(rp(sq), sk, sv, rp(sz)).reshape(mp, HEAD_W)
    zc = jnp.zeros((bsz, ML_HEADS, ML_DIM, ML_DIM), F32)
    zv = jnp.zeros((bsz, SAMPLE_PAD, ML_DIM), F32)
    b_p, mlc_p, mln_p, mlm_p = _mlstm(mq, mk, mv, mo, mz, mg, gate_bias, ml_gn0, zc, zv, zv, bsz, ML_CHUNK, ML_CHUNK)
    x1p = _outproj(a_p, b_p, xp, mod_p[0], w_out0, final_g, seq, tm, False)
    sbk_p, sbv_p = sk, sv

    sq, sk, sv, sz, mq, mk, mv, mo, mz, mg = _inproj(xs, mod_s[0], norm_g[0], w_even, widths_even, ms, ms)
    r3 = lambda u: u.reshape(db, SAMPLE_PAD, HEAD_W)
    a_s = _sb_sample(r3(sq), r3(sk), r3(sv), r3(sz), _pages_feature_major(cache_sb_k[0]),
                     _pages_feature_major(cache_sb_v[0]), page_table, t_real).reshape(ms, HEAD_W)
    n0 = jnp.pad(state_ml_n[0], ((0, 0), (0, SAMPLE_PAD - ML_HEADS), (0, 0)))
    m0 = jnp.broadcast_to(jnp.pad(state_ml_m[0], ((0, 0), (0, SAMPLE_PAD - ML_HEADS)))[:, :, None],
                          (db, SAMPLE_PAD, ML_DIM))
    b_s, mlc_s, mln_s, mlm_s = _mlstm(mq, mk, mv, mo, mz, mg, gate_bias, ml_gn0, state_ml_C[0], n0, m0,
                                      db, SAMPLE_PAD, t_real)
    x1s = _outproj(a_s, b_s, xs, mod_s[0], w_out0, final_g, ms, ms, False)
    sbk_s, sbv_s = r3(sk)[:, :t_real], r3(sv)[:, :t_real]

    kw = GLA_HEADS * GLA_DK
    g0 = 2 * kw + 2 * HEAD_W
    w_odd = jnp.concatenate([w_in_odd[0, :, :g0], w_in_odd[0, :, g0 + GLA_RANK:],
                             _pad_cols(w_in_odd[0, :, g0:g0 + GLA_RANK], PAIR_W)], axis=1).astype(BF16)
    widths_odd = (kw, kw) + (HEAD_W,) * 6 + (PAIR_W,)
    wg = jnp.pad(gla_w_g2[0], ((0, PAIR_W - GLA_RANK), (0, 0))).astype(BF16)
    bg = gla_b_g[0].reshape(1, kw)
    gla_gn0 = gla_gn[0].reshape(1, HEAD_W)
    w_out1 = w_out_odd[0].astype(BF16)

    c0 = 2 * kw + 3 * HEAD_W
    w_odd_tok = jnp.concatenate([w_odd[:, :c0], w_odd[:, c0 + 2 * HEAD_W:]], axis=1)
    w_odd_kv = w_odd[:, c0:c0 + 2 * HEAD_W].T
    gq, gk, gv, gz, dq, dz, gg, dk, dv = _inproj(x1p, mod_p[1], norm_g[1], w_odd_tok,
                                                 (kw, kw) + (HEAD_W,) * 4 + (PAIR_W,), seq, tm,
                                                 w_odd_kv, (HEAD_W, HEAD_W))
    s_zero = jnp.zeros((bsz, 2, PAIR_W, GLA_DV), F32)
    c_p, glas_p = _gla(gq, gk, gv, gz, gg, wg, bg, gla_gn0, s_zero, bsz, 256, GLA_SUB, 256)
    d_p = _moba_prompt(rp(dq), dk, dv, rp(dz)).reshape(mp, HEAD_W)
    y_p = _outproj(c_p, d_p, x1p, mod_p[1], w_out1, final_g, seq, tm, True)
    mbk_p, mbv_p = dk, dv

    gq, gk, gv, gz, dq, dk, dv, dz, gg = _inproj(x1s, mod_s[1], norm_g[1], w_odd, widths_odd, ms, ms)
    c_s, glas_s = _gla(gq, gk, gv, gz, gg, wg, bg, gla_gn0, state_gla_S[0].reshape(db, 2, PAIR_W, GLA_DV),
                       db, SAMPLE_PAD, SAMPLE_PAD, t_real)
    ck = _pages_feature_major(cache_mb_k[0])
    cv = _pages_feature_major(cache_mb_v[0])
    idx = _mb_select(r3(dq), ck, page_table)
    d_s = _mb_sample(r3(dq), r3(dk), r3(dv), r3(dz), idx, ck, cv, page_table, t_real).reshape(ms, HEAD_W)
    y_s = _outproj(c_s, d_s, x1s, mod_s[1], w_out1, final_g, ms, ms, True)
    mbk_s, mbv_s = r3(dk)[:, :t_real], r3(dv)[:, :t_real]

    kv_p = lambda u, h, d: jnp.transpose(u.reshape(1, bsz, h, d, seq), (0, 1, 4, 2, 3))
    kv_s = lambda u, h, d: u.reshape(1, db, t_real, h, d)
    return (y_p.reshape(bsz, seq, D_MODEL), y_s.reshape(db, SAMPLE_PAD, D_MODEL)[:, :t_real],
            kv_p(sbk_p, SB_HEADS, SB_DIM), kv_p(sbv_p, SB_HEADS, SB_DIM),
            kv_s(sbk_s, SB_HEADS, SB_DIM), kv_s(sbv_s, SB_HEADS, SB_DIM),
            mlc_p[None], mln_p[None, :, :ML_HEADS], mlm_p[None, :, :ML_HEADS, 0],
            mlc_s[None], mln_s[None, :, :ML_HEADS], mlm_s[None, :, :ML_HEADS, 0],
            glas_p.reshape(1, bsz, GLA_HEADS, GLA_DK, GLA_DV), glas_s.reshape(1, db, GLA_HEADS, GLA_DK, GLA_DV),
            kv_p(mbk_p, MB_HEADS, MB_DIM), kv_p(mbv_p, MB_HEADS, MB_DIM),
            kv_s(mbk_s, MB_HEADS, MB_DIM), kv_s(mbv_s, MB_HEADS, MB_DIM))
```

```python
import functools

import jax
import jax.numpy as jnp
from jax import lax
from jax.experimental import pallas as pl
from jax.experimental.pallas import tpu as pltpu

F32 = jnp.float32
BF16 = jnp.bfloat16

D_MODEL = 1024
EPS = 1e-6
PAGE = 128
SB_HEADS, SB_DIM = 8, 64
ML_HEADS, ML_DIM = 4, 128
ML_CHUNK = 256
GLA_HEADS, GLA_DK, GLA_DV, GLA_RANK, GLA_TAU = 4, 64, 128, 16, 16.0
MB_HEADS, MB_DIM, MB_BLOCK, MB_TOPK = 8, 64, 256, 3
HEAD_W = 512
PAIR_W = 128
SAMPLE_PAD = 8
NEG = -1e30
LOG2E = 1.4426950408889634
SB_SKIP = 88.0
SB_SKIP2 = SB_SKIP * 1.4426950408889634
GLA_SUB = 16


def _dot(a, b):
    return jnp.dot(a.astype(BF16), b.astype(BF16), preferred_element_type=F32)


def _dot_nt(a, b):
    return lax.dot_general(a.astype(BF16), b.astype(BF16), (((1,), (1,)), ((), ())), preferred_element_type=F32)


def _dot_tn(a, b):
    return lax.dot_general(a.astype(BF16), b.astype(BF16), (((0,), (0,)), ((), ())), preferred_element_type=F32)


def _softplus_tail(z):
    return jnp.log(1.0 + jnp.exp(-jnp.abs(z)))


def _log_sigmoid(z):
    return jnp.minimum(z, 0.0) - _softplus_tail(z)


def _sigmoid(z):
    return 1.0 / (1.0 + jnp.exp(-z))


def _silu(z):
    return z * _sigmoid(z)


def _iota(shape, axis):
    return lax.broadcasted_iota(jnp.int32, shape, axis)


def _params(sem, vmem_mb=None):
    kw = dict(dimension_semantics=sem)
    if vmem_mb is not None:
        kw["vmem_limit_bytes"] = vmem_mb * 1024 * 1024
    return pltpu.CompilerParams(**kw)


def _mod_kernel(c_ref, w_ref, b_ref, o_ref):
    o_ref[...] = _dot(_silu(c_ref[...]), w_ref[...]) + b_ref[...]


def _modulation(c_all, w_mod, b_mod):
    depth = w_mod.shape[0]
    rows = c_all.shape[0]
    return pl.pallas_call(
        _mod_kernel,
        grid=(depth, 3),
        in_specs=[
            pl.BlockSpec((rows, D_MODEL), lambda l, j: (0, 0)),
            pl.BlockSpec((None, D_MODEL, D_MODEL), lambda l, j: (l, 0, j)),
            pl.BlockSpec((None, 1, D_MODEL), lambda l, j: (l, 0, j)),
        ],
        out_specs=pl.BlockSpec((None, rows, D_MODEL), lambda l, j: (l, 0, j)),
        out_shape=jax.ShapeDtypeStruct((depth, rows, 3 * D_MODEL), F32),
        name="modulation",
        compiler_params=_params(("arbitrary", "arbitrary")),
    )(c_all, w_mod, b_mod.reshape(depth, 1, 3 * D_MODEL))


def _inproj_kernel(x_ref, mod_ref, g_ref, w_ref, *refs, widths, widths_t):
    x = x_ref[...]
    y = x * lax.rsqrt(jnp.mean(x * x, axis=-1, keepdims=True) + EPS) * g_ref[...]
    mod = mod_ref[...]
    h = y * (1.0 + mod[:, D_MODEL:2 * D_MODEL]) + mod[:, :D_MODEL]
    hb = h.astype(BF16)
    out_refs = refs[1:] if widths_t else refs
    off = 0
    for o_ref, wd in zip(out_refs, widths):
        o_ref[...] = jnp.dot(hb, w_ref[:, off:off + wd], preferred_element_type=F32)
        off += wd
    if widths_t:
        wt_ref = refs[0]
        ht = h.T.astype(BF16)
        off = 0
        for o_ref, wd in zip(out_refs[len(widths):], widths_t):
            o_ref[...] = jnp.dot(wt_ref[off:off + wd, :], ht, preferred_element_type=F32)
            off += wd


def _inproj(x, mod, g, w, widths, rows_per_group, tm, wt=None, widths_t=()):
    m = x.shape[0]
    r = mod.shape[1]
    n = w.shape[1]
    groups = m // rows_per_group
    per_group = rows_per_group // tm
    in_specs = [
        pl.BlockSpec((tm, D_MODEL), lambda i: (i, 0)),
        pl.BlockSpec((None, r, 3 * D_MODEL), lambda i: ((i * tm) // rows_per_group, 0, 0)),
        pl.BlockSpec((1, D_MODEL), lambda i: (0, 0)),
        pl.BlockSpec((D_MODEL, n), lambda i: (0, 0), pipeline_mode=pl.Buffered(1)),
    ]
    args = [x, mod, g.reshape(1, D_MODEL), w]
    if widths_t:
        in_specs.append(pl.BlockSpec(wt.shape, lambda i: (0, 0), pipeline_mode=pl.Buffered(1)))
        args.append(wt)
    return pl.pallas_call(
        functools.partial(_inproj_kernel, widths=widths, widths_t=widths_t),
        grid=(m // tm,),
        in_specs=in_specs,
        out_specs=([pl.BlockSpec((tm, wd), lambda i: (i, 0)) for wd in widths]
                   + [pl.BlockSpec((None, wd, tm), lambda i: (i // per_group, 0, i % per_group)) for wd in widths_t]),
        out_shape=([jax.ShapeDtypeStruct((m, wd), F32) for wd in widths]
                   + [jax.ShapeDtypeStruct((groups, wd, rows_per_group), F32) for wd in widths_t]),
        name="inproj",
        compiler_params=_params(("arbitrary",), 56),
    )(*args)


def _outproj_kernel(a_ref, b_ref, x_ref, mod_ref, w_ref, fg_ref, o_ref, *, final):
    y = _dot(a_ref[...], w_ref[:HEAD_W, :]) + _dot(b_ref[...], w_ref[HEAD_W:, :])
    xn = x_ref[...] + mod_ref[...][:, 2 * D_MODEL:] * y
    if final:
        xn = xn * lax.rsqrt(jnp.mean(xn * xn, axis=-1, keepdims=True) + EPS) * fg_ref[...]
    o_ref[...] = xn


def _outproj(a, b, x, mod, w, final_g, rows_per_group, tm, final):
    m = x.shape[0]
    r = mod.shape[1]
    return pl.pallas_call(
        functools.partial(_outproj_kernel, final=final),
        grid=(m // tm,),
        in_specs=[
            pl.BlockSpec((tm, HEAD_W), lambda i: (i, 0)),
            pl.BlockSpec((tm, HEAD_W), lambda i: (i, 0)),
            pl.BlockSpec((tm, D_MODEL), lambda i: (i, 0)),
            pl.BlockSpec((None, r, 3 * D_MODEL), lambda i: ((i * tm) // rows_per_group, 0, 0)),
            pl.BlockSpec((2 * HEAD_W, D_MODEL), lambda i: (0, 0)),
            pl.BlockSpec((1, D_MODEL), lambda i: (0, 0)),
        ],
        out_specs=pl.BlockSpec((tm, D_MODEL), lambda i: (i, 0)),
        out_shape=jax.ShapeDtypeStruct((m, D_MODEL), F32),
        name="outproj",
        compiler_params=_params(("arbitrary",)),
    )(a, b, x, mod, w, final_g.reshape(1, D_MODEL))


SB_GROUP = 4


def _later_matrix(n):
    return jnp.where(_iota((n, n), 0) > _iota((n, n), 1), 1.0, 0.0).astype(BF16)


def _sb_rows_tile(z2, valid, carry, later):
    sp = jnp.maximum(z2, 0.0) + jnp.log2(1.0 + jnp.exp2(-jnp.abs(z2)))
    if valid is not None:
        sp = jnp.where(valid, sp, 0.0)
    hi = sp.astype(BF16)
    lo = (sp - hi.astype(F32)).astype(BF16)
    t = sp + jnp.dot(hi, later, preferred_element_type=F32) + jnp.dot(lo, later, preferred_element_type=F32) + carry
    w = jnp.exp2(z2 - t)
    if valid is not None:
        w = jnp.where(valid, w, 0.0)
    return w, t[:, 0:1]


def _sb_prompt_kernel(q_ref, z_ref, kt_hbm, vt_hbm, o_ref, kt_sc, vt_sc, sem, acc_sc,
                      qst_sc, z_sc, sp_sc, hilo_sc, carry_sc, *, tq):
    b = pl.program_id(0)
    i = pl.program_id(1)

    @pl.when(i == 0)
    def _():
        ck = pltpu.make_async_copy(kt_hbm.at[b], kt_sc, sem.at[0])
        cv = pltpu.make_async_copy(vt_hbm.at[b], vt_sc, sem.at[1])
        ck.start()
        cv.start()
        ck.wait()
        cv.wait()

    n_pairs = SB_HEADS // 2
    rows_all = SB_HEADS * tq
    later = _later_matrix(tq)
    lane_head = _iota((tq, PAIR_W), 1) // SB_DIM
    row_head = _iota((PAIR_W, tq), 0) // SB_DIM
    q = q_ref[...] * (SB_DIM ** -0.5 * LOG2E)
    for p in range(n_pairs):
        qp = q[:, p * PAIR_W:(p + 1) * PAIR_W]
        for hh in range(2):
            qst_sc[p, hh * tq:(hh + 1) * tq, :] = jnp.where(lane_head == hh, qp, 0.0).astype(BF16)
    strictly_before = _iota((2 * tq, tq), 1) < _iota((2 * tq, tq), 0) % tq

    def tile(j, diag):
        start = pl.multiple_of(j * tq, tq)
        for g in range(n_pairs // SB_GROUP):
            pairs = range(g * SB_GROUP, (g + 1) * SB_GROUP)
            for p in pairs:
                rs = slice(p * 2 * tq, (p + 1) * 2 * tq)
                ktp = kt_sc[p * PAIR_W:(p + 1) * PAIR_W, pl.ds(start, tq)].astype(BF16)
                z2 = jnp.dot(qst_sc[p], ktp, preferred_element_type=F32)
                sp = jnp.maximum(z2, 0.0) + jnp.log2(1.0 + jnp.exp2(-jnp.abs(z2)))
                if diag:
                    sp = jnp.where(strictly_before, sp, 0.0)
                hi = sp.astype(BF16)
                z_sc[rs, :] = z2
                sp_sc[rs, :] = sp
                hilo_sc[rs, :] = hi
                hilo_sc[p * 2 * tq + rows_all:(p + 1) * 2 * tq + rows_all, :] = (sp - hi.astype(F32)).astype(BF16)
            g0 = g * SB_GROUP * 2 * tq
            g1 = (g + 1) * SB_GROUP * 2 * tq
            cum_hi = jnp.dot(hilo_sc[g0:g1, :], later, preferred_element_type=F32)
            cum_lo = jnp.dot(hilo_sc[rows_all + g0:rows_all + g1, :], later, preferred_element_type=F32)
            for p in pairs:
                rs = slice(p * 2 * tq, (p + 1) * 2 * tq)
                rg = slice(p * 2 * tq - g0, (p + 1) * 2 * tq - g0)
                t = sp_sc[rs, :] + cum_hi[rg] + cum_lo[rg]
                if not diag:
                    t = t + carry_sc[rs, :]
                w = jnp.exp2((z_sc[rs, :] - t).astype(BF16))
                if diag:
                    w = jnp.where(strictly_before, w, jnp.zeros_like(w))
                carry_sc[rs, :] = t[:, 0:1]
                vtp = vt_sc[p * PAIR_W:(p + 1) * PAIR_W, pl.ds(start, tq)]
                w2 = jnp.concatenate([w[0:tq], w[tq:2 * tq]], axis=1)
                v2 = jnp.concatenate([jnp.where(row_head == hh, vtp, 0.0) for hh in range(2)], axis=1)
                upd = _dot_nt(w2, v2)
                if diag:
                    acc_sc[:, p * PAIR_W:(p + 1) * PAIR_W] = upd
                else:
                    acc_sc[:, p * PAIR_W:(p + 1) * PAIR_W] += upd
        return jnp.min(carry_sc[...])

    least = tile(i, True)

    def cond(c):
        return (c[0] >= 0) & (c[1] < SB_SKIP2)

    def body(c):
        return c[0] - 1, tile(c[0], False)

    lax.while_loop(cond, body, (i - 1, least))
    o_ref[...] = acc_sc[...] * _silu(z_ref[...])


def _sb_prompt(q, kt, vt, z, tq=128):
    b, s, _ = q.shape
    blk = pl.BlockSpec((None, tq, HEAD_W), lambda bi, i: (bi, i, 0))
    hbm = pl.BlockSpec(memory_space=pl.ANY)
    return pl.pallas_call(
        functools.partial(_sb_prompt_kernel, tq=tq),
        grid=(b, s // tq),
        in_specs=[blk, blk, hbm, hbm],
        out_specs=blk,
        out_shape=jax.ShapeDtypeStruct((b, s, HEAD_W), F32),
        scratch_shapes=[pltpu.VMEM((HEAD_W, s), F32),
                        pltpu.VMEM((HEAD_W, s), F32),
                        pltpu.SemaphoreType.DMA((2,)),
                        pltpu.VMEM((tq, HEAD_W), F32),
                        pltpu.VMEM((SB_HEADS // 2, 2 * tq, PAIR_W), BF16),
                        pltpu.VMEM((SB_HEADS * tq, tq), F32),
                        pltpu.VMEM((SB_HEADS * tq, tq), F32),
                        pltpu.VMEM((2 * SB_HEADS * tq, tq), BF16),
                        pltpu.VMEM((SB_HEADS * tq, 1), F32)],
        name="sb_prompt",
        compiler_params=_params(("arbitrary", "arbitrary"), 48),
    )(q, z, kt, vt)


def _tile_rows(x, reps):
    return jnp.concatenate([x] * reps, axis=0)


def _block_diag_queries(q, n_cols, head_dim):
    qt = _tile_rows(q, n_cols // SAMPLE_PAD)
    row_head = _iota((n_cols, HEAD_W), 0) // SAMPLE_PAD
    lane_head = _iota((n_cols, HEAD_W), 1) // head_dim
    return jnp.where(row_head == lane_head, qt, 0.0)


def _gather_heads(res, head_dim):
    lane_head = _iota((SAMPLE_PAD, HEAD_W), 1) // head_dim
    out = jnp.zeros((SAMPLE_PAD, HEAD_W), F32)
    for h in range(HEAD_W // head_dim):
        out = out + jnp.where(lane_head == h, res[h * SAMPLE_PAD:(h + 1) * SAMPLE_PAD, :], 0.0)
    return out


def _sb_sample_kernel(pt_ref, q_ref, kn_ref, vn_ref, z_ref, ck_hbm, cv_hbm, o_ref,
                      kbuf, vbuf, sem, acc_ref, kpad, vpad, *, n_pages, t_real):
    b = pl.program_id(0)
    nq = PAIR_W

    def copies(p, slot, req=b):
        phys = pt_ref[req * n_pages + p]
        return (pltpu.make_async_copy(ck_hbm.at[phys], kbuf.at[slot], sem.at[0, slot]),
                pltpu.make_async_copy(cv_hbm.at[phys], vbuf.at[slot], sem.at[1, slot]))

    def start(p, slot, req=b):
        for cp in copies(p, slot, req):
            cp.start()

    def wait(p, slot):
        for cp in copies(p, slot):
            cp.wait()

    def first_pages(req):
        start(n_pages - 1, (n_pages - 1) % 2, req)
        start(n_pages - 2, (n_pages - 2) % 2, req)

    @pl.when(b == 0)
    def _():
        first_pages(0)
        kpad[...] = jnp.zeros_like(kpad)
        vpad[...] = jnp.zeros_like(vpad)

    kpad[0:SAMPLE_PAD, :] = kn_ref[...]
    vpad[0:SAMPLE_PAD, :] = vn_ref[...]

    qbd = _block_diag_queries(q_ref[...] * (SB_DIM ** -0.5 * LOG2E), nq, SB_DIM).astype(BF16)
    later = _later_matrix(PAGE)
    qrow = _iota((nq, PAGE), 0)
    key = _iota((nq, PAGE), 1)
    valid_new = (key < qrow % SAMPLE_PAD) & (key < t_real)
    row1 = _iota((nq, 1), 0)
    real_row = (row1 < SB_HEADS * SAMPLE_PAD) & (row1 % SAMPLE_PAD < t_real)

    w, carry = _sb_rows_tile(_dot_nt(qbd, kpad[...]), valid_new, jnp.zeros((nq, 1), F32), later)
    acc_ref[...] = _dot(w, vpad[...])

    def least(c):
        return jnp.min(jnp.where(real_row, c, jnp.inf))

    def cond(c):
        return (c[0] >= 0) & (c[2] < SB_SKIP2)

    def body(c):
        p, carry, _ = c
        slot = p % 2
        wait(p, slot)
        w, cn = _sb_rows_tile(jnp.dot(qbd, kbuf[slot].astype(BF16), preferred_element_type=F32), None, carry, later)
        acc_ref[...] += _dot_nt(w, vbuf[slot])

        @pl.when(p >= 2)
        def _():
            start(p - 2, slot)

        return p - 1, cn, least(cn)

    p_end, _, _ = lax.while_loop(cond, body, (jnp.int32(n_pages - 1), carry, least(carry)))

    @pl.when(p_end >= 0)
    def _():
        wait(p_end, p_end % 2)

    @pl.when(p_end >= 1)
    def _():
        wait(p_end - 1, (p_end - 1) % 2)

    @pl.when(b + 1 < pl.num_programs(0))
    def _():
        first_pages(b + 1)

    o_ref[...] = _gather_heads(acc_ref[...], SB_DIM) * _silu(z_ref[...])


def _sb_sample(q, kn, vn, z, cache_k, cache_v, page_table, t_real):
    nb = q.shape[0]
    n_pages = page_table.shape[1]
    blk = pl.BlockSpec((None, SAMPLE_PAD, HEAD_W), lambda bi, pt: (bi, 0, 0))
    hbm = pl.BlockSpec(memory_space=pl.ANY)
    return pl.pallas_call(
        functools.partial(_sb_sample_kernel, n_pages=n_pages, t_real=t_real),
        grid_spec=pltpu.PrefetchScalarGridSpec(
            num_scalar_prefetch=1,
            grid=(nb,),
            in_specs=[blk, blk, blk, blk, hbm, hbm],
            out_specs=blk,
            scratch_shapes=[
                pltpu.VMEM((2, HEAD_W, PAGE), F32),
                pltpu.VMEM((2, HEAD_W, PAGE), F32),
                pltpu.SemaphoreType.DMA((2, 2)),
                pltpu.VMEM((PAIR_W, HEAD_W), F32),
                pltpu.VMEM((PAGE, HEAD_W), F32),
                pltpu.VMEM((PAGE, HEAD_W), F32),
            ],
        ),
        out_shape=jax.ShapeDtypeStruct((nb, SAMPLE_PAD, HEAD_W), F32),
        name="sb_sample",
        compiler_params=_params(("arbitrary",)),
    )(page_table.reshape(-1), q, kn, vn, z, cache_k, cache_v)


def _mlstm_kernel(q_ref, k_ref, v_ref, og_ref, z_ref, gate_ref, bias_ref, gn_ref, c0_ref, n0_ref, m0_ref,
                  out_ref, c_out, n_out, m_out, c_sc, n_sc, m_sc, *, chunk, valid, per_step):
    ci = pl.program_id(1)

    @pl.when(ci == 0)
    def _():
        c_sc[...] = c0_ref[...]
        n_sc[...] = n0_ref[...]
        m_sc[...] = m0_ref[...]

    for bb in range(per_step):
        _mlstm_chunk(q_ref.at[bb], k_ref.at[bb], v_ref.at[bb], og_ref.at[bb], z_ref.at[bb], gate_ref.at[bb],
                     bias_ref, gn_ref, out_ref.at[bb], c_sc.at[bb], n_sc.at[bb], m_sc.at[bb], chunk, valid)

    @pl.when(ci == pl.num_programs(1) - 1)
    def _():
        c_out[...] = c_sc[...]
        n_out[...] = n_sc[...]
        m_out[...] = m_sc[...]


def _mlstm_chunk(q_ref, k_ref, v_ref, og_ref, z_ref, gate_ref, bias_ref, gn_ref, out_ref, c_sc, n_sc, m_sc,
                 chunk, valid):
    pre = gate_ref[...] + bias_ref[...]
    lf = pltpu.roll(_log_sigmoid(pre), PAIR_W - ML_HEADS, 1)
    ig = pre
    if valid < chunk:
        live = _iota((chunk, PAIR_W), 0) < valid
        lf = jnp.where(live, lf, 0.0)
        ig = jnp.where(live, ig, -jnp.inf)
    r = _iota((chunk, chunk), 0)
    c = _iota((chunk, chunk), 1)
    tril = c <= r
    lf_hi = lf.astype(BF16)
    bc2 = jnp.dot(jnp.where(tril, 1.0, 0.0).astype(BF16),
                  jnp.concatenate([lf_hi, (lf - lf_hi.astype(F32)).astype(BF16)], axis=1), preferred_element_type=F32)
    bc = bc2[:, :PAIR_W] + bc2[:, PAIR_W:]
    a = ig - bc
    a_t = a.T
    q_all = q_ref[...]
    k_all = k_ref[...] * (ML_DIM ** -0.5)
    v_all = v_ref[...]
    rows_st = ML_HEADS * chunk
    q_st = jnp.where(_iota((rows_st, HEAD_W), 0) // chunk == _iota((rows_st, HEAD_W), 1) // ML_DIM,
                     jnp.concatenate([q_all] * ML_HEADS, axis=0), 0.0).astype(BF16)
    qk_st = _dot_nt(q_st, k_all)
    qc_st = _dot(q_st, c_sc[...].reshape(ML_HEADS * ML_DIM, ML_DIM))
    wts, ms, inters, wks, decays = [], [], [], [], []
    for h in range(ML_HEADS):
        sl = slice(h * ML_DIM, (h + 1) * ML_DIM)
        bc_h = bc[:, h:h + 1]
        a_row = a_t[h:h + 1, :]
        m0 = m_sc[h:h + 1, 0:1]
        cm = jnp.max(jnp.where(tril, a_row, -jnp.inf), axis=-1, keepdims=True)
        m_h = jnp.maximum(bc_h + m0, cm + bc_h)
        dmat = jnp.exp(jnp.where(tril, bc_h - m_h + a_row, -jnp.inf))
        wts.append(dmat * qk_st[h * chunk:(h + 1) * chunk])
        ms.append(m_h)
        inters.append(jnp.exp(bc_h + m0 - m_h))
        bc_l = bc_h[chunk - 1:chunk, :]
        m_l = m_h[chunk - 1:chunk, :]
        decays.append(jnp.exp(bc_l + m0 - m_l))
        wks.append(jnp.exp(bc_l + a[:, h:h + 1] - m_l) * k_all[:, sl])
        m_sc[h:h + 1, :] = jnp.broadcast_to(m_l, (1, PAIR_W))
    wv_st = _dot(jnp.concatenate(wts, axis=0), v_all)
    wk_all = jnp.concatenate(wks, axis=1)
    upd = _dot_tn(wk_all, v_all)
    for h in range(ML_HEADS):
        sl = slice(h * ML_DIM, (h + 1) * ML_DIM)
        rs = slice(h * chunk, (h + 1) * chunk)
        n_h = n_sc[h:h + 1, :]
        num = inters[h] * qc_st[rs] + wv_st[rs, sl]
        den = (inters[h] * jnp.sum(q_all[:, sl] * n_h, axis=-1, keepdims=True)
               + jnp.sum(wts[h], axis=-1, keepdims=True))
        hv = num / jnp.maximum(jnp.abs(den), jnp.exp(-ms[h]))
        hn = hv * lax.rsqrt(jnp.mean(hv * hv, axis=-1, keepdims=True) + EPS) * gn_ref[:, sl]
        out_ref[:, sl] = hn * _sigmoid(og_ref[:, sl]) * _silu(z_ref[:, sl])
        c_sc[h] = decays[h] * c_sc[h] + upd[sl, sl]
        n_sc[h:h + 1, :] = decays[h] * n_h + jnp.sum(wks[h], axis=0, keepdims=True)


ML_PER_STEP = 1


def _mlstm(q, k, v, og, z, gates, bias, gn, c0, n0, m0, nb, chunk, valid):
    m = q.shape[0]
    t = m // nb
    n_chunks = t // chunk
    per = ML_PER_STEP
    tok = pl.BlockSpec((per, chunk, HEAD_W), lambda g, ci: (g, ci, 0))
    st_c = pl.BlockSpec((per, ML_HEADS, ML_DIM, ML_DIM), lambda g, ci: (g, 0, 0, 0))
    st_v = pl.BlockSpec((per, SAMPLE_PAD, ML_DIM), lambda g, ci: (g, 0, 0))
    r3 = lambda u: u.reshape(nb, t, u.shape[-1])
    out, c1, n1, m1 = pl.pallas_call(
        functools.partial(_mlstm_kernel, chunk=chunk, valid=valid, per_step=per),
        grid=(nb // per, n_chunks),
        in_specs=[tok, tok, tok, tok, tok,
                  pl.BlockSpec((per, chunk, PAIR_W), lambda g, ci: (g, ci, 0)),
                  pl.BlockSpec((1, PAIR_W), lambda g, ci: (0, 0)),
                  pl.BlockSpec((1, HEAD_W), lambda g, ci: (0, 0)),
                  st_c, st_v, st_v],
        out_specs=[tok, st_c, st_v, st_v],
        out_shape=[jax.ShapeDtypeStruct((nb, t, HEAD_W), F32),
                   jax.ShapeDtypeStruct((nb, ML_HEADS, ML_DIM, ML_DIM), F32),
                   jax.ShapeDtypeStruct((nb, SAMPLE_PAD, ML_DIM), F32),
                   jax.ShapeDtypeStruct((nb, SAMPLE_PAD, ML_DIM), F32)],
        scratch_shapes=[pltpu.VMEM((per, ML_HEADS, ML_DIM, ML_DIM), F32),
                        pltpu.VMEM((per, SAMPLE_PAD, ML_DIM), F32),
                        pltpu.VMEM((per, SAMPLE_PAD, ML_DIM), F32)],
        name="mlstm",
        compiler_params=_params(("arbitrary", "arbitrary")),
    )(r3(q), r3(k), r3(v), r3(og), r3(z), r3(gates), bias, gn, c0, n0, m0)
    return out.reshape(m, HEAD_W), c1, n1, m1


def _gla_kernel(q_ref, k_ref, v_ref, z_ref, gg_ref, wg_ref, bg_ref, gn_ref, s0_ref, out_ref, s_out,
                st_sc, kb, gb, vb, qa_sc, kc_sc, dec_sc, oi_sc, *, tb, sub, valid):
    ti = pl.program_id(1)
    kw = GLA_HEADS * GLA_DK

    @pl.when(ti == 0)
    def _():
        for p in range(2):
            st_sc[p] = s0_ref[p].T
        kb[0:sub, :] = jnp.zeros((sub, kw), F32)
        gb[0:sub, :] = jnp.zeros((sub, kw), F32)
        vb[0:sub, :] = jnp.zeros((sub, HEAD_W), F32)

    lg = _log_sigmoid(_dot(gg_ref[...], wg_ref[...]) + bg_ref[...]) * (1.0 / GLA_TAU)
    k = k_ref[...]
    if valid < tb:
        live = _iota((tb, kw), 0) < valid
        lg = jnp.where(live, lg, 0.0)
        k = jnp.where(live, k, 0.0)
    r = _iota((tb, tb), 0)
    c = _iota((tb, tb), 1)
    same = (r // sub) == (c // sub)
    sums = jnp.concatenate([jnp.where(same & (c <= r), 1.0, 0.0), jnp.where(same, 1.0, 0.0)], axis=0).astype(BF16)
    lg_hi = lg.astype(BF16)
    parts = jnp.dot(sums, jnp.concatenate([lg_hi, (lg - lg_hi.astype(F32)).astype(BF16)], axis=1),
                    preferred_element_type=F32)
    g = parts[:tb, :kw] + parts[:tb, kw:]
    gend = parts[tb:, :kw] + parts[tb:, kw:]
    q = q_ref[...] * (GLA_DK ** -0.5)
    v = v_ref[...]
    qa_sc[...] = q * jnp.exp(g)
    kc_sc[...] = k * jnp.exp(gend - g)
    dec_sc[...] = jnp.exp(gend)
    kb[sub:sub + tb, :] = k
    gb[sub:sub + tb, :] = g
    vb[sub:sub + tb, :] = v

    head_spread = jnp.where(_iota((kw, HEAD_W), 0) // GLA_DK == _iota((kw, HEAD_W), 1) // GLA_DV, 1.0, 0.0).astype(BF16)
    rmod = _iota((tb, 1), 0) % sub
    oi = jnp.zeros((tb, HEAD_W), F32)
    for d in range(sub):
        lo = sub - d
        dec = jnp.exp(jnp.where(rmod >= d, g - gb[lo:lo + tb, :], NEG))
        oi = oi + _dot(q * kb[lo:lo + tb, :] * dec, head_spread) * vb[lo:lo + tb, :]
    oi_sc[...] = oi

    lane_head = _iota((2 * sub, PAIR_W), 1) // GLA_DK
    row_head = _iota((2 * sub, PAIR_W), 0) // sub
    for i in range(tb // sub):
        rows = slice(i * sub, (i + 1) * sub)
        qa = qa_sc[rows, :]
        kc = kc_sc[rows, :]
        dec = dec_sc[rows, :][0:1, :]
        for p in range(2):
            ps = slice(p * PAIR_W, (p + 1) * PAIR_W)
            vs = slice(2 * p * GLA_DV, (2 * p + 2) * GLA_DV)
            st = st_sc[p]
            own = lane_head == row_head
            o2 = _dot_nt(jnp.where(own, jnp.concatenate([qa[:, ps]] * 2, axis=0), 0.0), st)
            v2 = jnp.concatenate([v_ref[rows, vs][:, :GLA_DV], v_ref[rows, vs][:, GLA_DV:]], axis=0)
            st_sc[p] = st * dec[:, ps] + _dot_tn(v2, jnp.where(own, jnp.concatenate([kc[:, ps]] * 2, axis=0), 0.0))
            for hh in range(2):
                hs = slice((2 * p + hh) * GLA_DV, (2 * p + hh + 1) * GLA_DV)
                o = o2[hh * sub:(hh + 1) * sub] + oi_sc[rows, hs]
                on = o * lax.rsqrt(jnp.mean(o * o, axis=-1, keepdims=True) + EPS) * gn_ref[:, hs]
                out_ref[rows, hs] = on * _silu(z_ref[rows, hs])

    @pl.when(ti == pl.num_programs(1) - 1)
    def _():
        for p in range(2):
            s_out[p] = st_sc[p].T


def _gla(q, k, v, z, gg, wg, bg, gn, s0, nb, tb, sub, valid):
    m = q.shape[0]
    nt = m // (nb * tb)
    kw = GLA_HEADS * GLA_DK
    tok_k = pl.BlockSpec((tb, kw), lambda b, t: (b * nt + t, 0))
    tok_v = pl.BlockSpec((tb, HEAD_W), lambda b, t: (b * nt + t, 0))
    st = pl.BlockSpec((None, 2, PAIR_W, GLA_DV), lambda b, t: (b, 0, 0, 0))
    return pl.pallas_call(
        functools.partial(_gla_kernel, tb=tb, sub=sub, valid=valid),
        grid=(nb, nt),
        in_specs=[tok_k, tok_k, tok_v, tok_v,
                  pl.BlockSpec((tb, PAIR_W), lambda b, t: (b * nt + t, 0)),
                  pl.BlockSpec((PAIR_W, kw), lambda b, t: (0, 0)),
                  pl.BlockSpec((1, kw), lambda b, t: (0, 0)),
                  pl.BlockSpec((1, HEAD_W), lambda b, t: (0, 0)),
                  st],
        out_specs=[tok_v, st],
        out_shape=[jax.ShapeDtypeStruct((m, HEAD_W), F32),
                   jax.ShapeDtypeStruct((nb, 2, PAIR_W, GLA_DV), F32)],
        scratch_shapes=[pltpu.VMEM((2, GLA_DV, PAIR_W), F32),
                        pltpu.VMEM((sub + tb, kw), F32),
                        pltpu.VMEM((sub + tb, kw), F32),
                        pltpu.VMEM((sub + tb, HEAD_W), F32),
                        pltpu.VMEM((tb, kw), F32),
                        pltpu.VMEM((tb, kw), F32),
                        pltpu.VMEM((tb, kw), F32),
                        pltpu.VMEM((tb, HEAD_W), F32)],
        name="gla",
        compiler_params=_params(("arbitrary", "arbitrary")),
    )(q, k, v, z, gg, wg, bg, gn, s0)


def _top3_rows(s, n_valid):
    nb = s.shape[0]
    nrow = _iota(s.shape, 0)
    live = nrow < n_valid
    s = jnp.where(live, s, -jnp.inf)
    chosen = jnp.zeros(s.shape, F32)
    picks = []
    for _ in range(MB_TOPK):
        mx = jnp.max(s, axis=0, keepdims=True)
        idx = jnp.min(jnp.where(s == mx, nrow, nb), axis=0, keepdims=True)
        pick = nrow == idx
        chosen = jnp.where(pick, jnp.where(live, 1.0, 0.0), chosen)
        s = jnp.where(pick, -jnp.inf, s)
        picks.append(idx)
    return picks, chosen


KEY_CHUNK = 64
MB_TRIP = 4
SUM_ROWS = 16


def _moba_prompt_kernel(q_ref, kt_ref, vt_ref, z_ref, o_ref,
                        ktok_sc, vtbd_sc, km_sc, bias_sc, sel2_sc, sel_sc, qtm_sc, p_sc, acc_sc, m_sc, *score_bufs,
                        tq, nblk):
    s_bufs, cm_bufs = score_bufs[:MB_TRIP + 1], score_bufs[MB_TRIP + 1:]
    p = pl.program_id(1)
    i = pl.program_id(2)
    slope2 = [jnp.exp2(-(jnp.full((1, tq), 2 * p + hh + 1, jnp.int32)).astype(F32)) * LOG2E for hh in range(2)]

    row_head = _iota((PAIR_W, tq), 0) // MB_DIM

    @pl.when(i == 0)
    def _():
        lane = _iota((PAIR_W, PAIR_W), 1)
        feat_head = _iota((PAIR_W, MB_BLOCK), 0) // MB_DIM
        km = jnp.zeros((PAIR_W, PAIR_W), F32)
        for n in range(nblk):
            cols = slice(n * MB_BLOCK, (n + 1) * MB_BLOCK)
            blk = kt_ref[:, cols]
            ktok_sc[cols, :] = blk.T.astype(BF16)
            km = jnp.where(lane == n, jnp.mean(blk, axis=1, keepdims=True), km)
            vblk = vt_ref[:, cols]
            ones_row = _iota((SUM_ROWS, MB_BLOCK), 0)
            for hh in range(2):
                vtbd_sc[0:PAIR_W, (2 * n + hh) * MB_BLOCK:(2 * n + hh + 1) * MB_BLOCK] = (
                    jnp.where(feat_head == hh, vblk, 0.0).astype(BF16))
                vtbd_sc[PAIR_W:PAIR_W + SUM_ROWS, (2 * n + hh) * MB_BLOCK:(2 * n + hh + 1) * MB_BLOCK] = (
                    jnp.where(ones_row == hh, 1.0, 0.0).astype(BF16))
        km_t = km.T
        blk_head = _iota((PAIR_W, PAIR_W), 1) // MB_DIM
        for hh in range(2):
            km_sc[hh * nblk:(hh + 1) * nblk, :] = jnp.where(blk_head == hh, km_t, 0.0)[0:nblk]
        key_off = _iota((MB_BLOCK, tq), 0).astype(F32)
        for hh in range(2):
            bias_sc[hh] = slope2[hh] * key_off

    q_t = q_ref[...].T
    s_sel = jnp.dot(km_sc[...], q_t, precision=lax.Precision.HIGHEST, preferred_element_type=F32)
    for hh in range(2):
        _, chosen = _top3_rows(s_sel[hh * nblk:(hh + 1) * nblk], i)
        sel2_sc[...] = chosen
        for n in range(nblk):
            sel_sc[hh, n] = sel2_sc[n:n + 1, :]
        qtm_sc[:, hh * tq:(hh + 1) * tq] = jnp.where(row_head == hh, q_t * (MB_DIM ** -0.5 * LOG2E), 0.0).astype(BF16)

    n_chunks = MB_BLOCK // KEY_CHUNK

    bufs = tuple((s_bufs[j], cm_bufs[j]) for j in range(MB_TRIP + 1))
    acc_row = _iota((PAIR_W + SUM_ROWS, tq), 0)
    acc_head0 = (acc_row < MB_DIM) | (acc_row == PAIR_W)

    def scores(n, buf, diag):
        s_ref, cm_ref = bufs[buf]
        start = pl.multiple_of(n * MB_BLOCK, MB_BLOCK)
        s = jnp.dot(ktok_sc[pl.ds(start, MB_BLOCK), :], qtm_sc[...], preferred_element_type=F32)
        for hh in range(2):
            cm = None
            for kc in range(n_chunks):
                rows = slice(kc * KEY_CHUNK, (kc + 1) * KEY_CHUNK)
                sc = s[rows, hh * tq:(hh + 1) * tq] + bias_sc[hh, rows]
                if diag:
                    key = _iota((KEY_CHUNK, tq), 0) + kc * KEY_CHUNK
                    sc = jnp.where(key <= _iota((KEY_CHUNK, tq), 1), sc, NEG)
                s_ref[hh, rows] = sc
                c = jnp.max(sc, axis=0, keepdims=True)
                cm = c if cm is None else jnp.maximum(cm, c)
            cm_ref[hh] = cm

    def attend(n, buf, diag):
        s_ref, cm_ref = bufs[buf]
        alphas = []
        for hh in range(2):
            cm = cm_ref[hh]
            if diag:
                m_new = cm
                shift = cm
            else:
                c = slope2[hh] * ((n - i) * MB_BLOCK).astype(F32)
                ok = sel_sc[hh, n] > 0.5
                m_old = m_sc[hh]
                m_new = jnp.maximum(m_old, jnp.where(ok, cm + c, NEG))
                shift = jnp.where(ok, m_new - c, -NEG)
                alphas.append(jnp.exp2(m_old - m_new))
            for kc in range(n_chunks):
                rows = slice(kc * KEY_CHUNK, (kc + 1) * KEY_CHUNK)
                p_sc[hh * MB_BLOCK + kc * KEY_CHUNK:hh * MB_BLOCK + (kc + 1) * KEY_CHUNK, :] = (
                    jnp.exp2((s_ref[hh, rows] - shift).astype(BF16)))
            m_sc[hh] = m_new
        vt = vtbd_sc[:, pl.ds(pl.multiple_of(n * (2 * MB_BLOCK), 2 * MB_BLOCK), 2 * MB_BLOCK)]
        pv = jnp.dot(vt, p_sc[...], preferred_element_type=F32)
        if diag:
            acc_sc[...] = pv
        else:
            acc_sc[...] = acc_sc[...] * jnp.where(acc_head0, alphas[0], alphas[1]) + pv

    last = nblk - 1
    scores(i, MB_TRIP, True)
    for j in range(MB_TRIP):
        scores(min(j, last), j, False)
    attend(i, MB_TRIP, True)

    def body(k, _):
        n0 = MB_TRIP * k
        for j in range(MB_TRIP):
            attend(jnp.minimum(n0 + j, last), j, False)
            scores(jnp.minimum(n0 + j + MB_TRIP, last), j, False)
        return 0

    lax.fori_loop(0, (i + MB_TRIP - 1) // MB_TRIP, body, 0)
    inv = [1.0 / acc_sc[PAIR_W + hh:PAIR_W + hh + 1, :] for hh in range(2)]
    out_t = acc_sc[0:PAIR_W, :] * jnp.where(row_head == 0, inv[0], inv[1])
    o_ref[...] = out_t.T * _silu(z_ref[...])


def _moba_prompt(q, kt, vt, z):
    b, s, _ = q.shape
    tq = MB_BLOCK
    nblk = s // MB_BLOCK
    assert nblk <= PAIR_W
    blk_q = pl.BlockSpec((None, tq, PAIR_W), lambda bi, p, i: (bi, i, p))
    blk_kv = pl.BlockSpec((None, PAIR_W, s), lambda bi, p, i: (bi, p, 0))
    return pl.pallas_call(
        functools.partial(_moba_prompt_kernel, tq=tq, nblk=nblk),
        grid=(b, HEAD_W // PAIR_W, s // tq),
        in_specs=[blk_q, blk_kv, blk_kv, blk_q],
        out_specs=blk_q,
        out_shape=jax.ShapeDtypeStruct((b, s, HEAD_W), F32),
        scratch_shapes=[pltpu.VMEM((s, PAIR_W), BF16),
                        pltpu.VMEM((PAIR_W + SUM_ROWS, 2 * s), BF16),
                        pltpu.VMEM((2 * nblk, PAIR_W), F32),
                        pltpu.VMEM((2, MB_BLOCK, tq), F32),
                        pltpu.VMEM((nblk, tq), F32),
                        pltpu.VMEM((2, nblk, 1, tq), F32),
                        pltpu.VMEM((PAIR_W, 2 * tq), BF16),
                        pltpu.VMEM((2 * MB_BLOCK, tq), BF16),
                        pltpu.VMEM((PAIR_W + SUM_ROWS, tq), F32),
                        pltpu.VMEM((2, 1, tq), F32)]
        + [pltpu.VMEM((2, MB_BLOCK, tq), F32)] * (MB_TRIP + 1)
        + [pltpu.VMEM((2, 1, tq), F32)] * (MB_TRIP + 1),
        name="moba_prompt",
        compiler_params=_params(("arbitrary", "arbitrary", "arbitrary")),
    )(q, kt, vt, z)


_MEAN_PAGES = 64


def _top3_lanes(s, n_valid):
    width = s.shape[1]
    lane = _iota(s.shape, 1)
    s = jnp.where(lane < n_valid, s, -jnp.inf)
    picks = []
    for _ in range(MB_TOPK):
        mx = jnp.max(s, axis=1, keepdims=True)
        idx = jnp.min(jnp.where(s == mx, lane, width), axis=1, keepdims=True)
        s = jnp.where(lane == idx, -jnp.inf, s)
        picks.append(idx)
    return picks


def _mb_select_kernel(pt_ref, *refs, n_blocks):
    pages = refs[:_MEAN_PAGES]
    q_ref, idx_ref, km_sc = refs[_MEAN_PAGES:]
    s = pl.program_id(1)
    per_blk = MB_BLOCK // PAGE
    blk_per_step = _MEAN_PAGES // per_blk

    @pl.when(s == 0)
    def _():
        km_sc[...] = jnp.zeros_like(km_sc)

    lane = _iota((HEAD_W, PAIR_W), 1)
    km = km_sc[...]
    for j in range(blk_per_step):
        both = sum(pages[per_blk * j + e][...] for e in range(per_blk))
        km = jnp.where(lane == s * blk_per_step + j, jnp.sum(both, axis=1, keepdims=True) * (1.0 / MB_BLOCK), km)
    km_sc[...] = km

    @pl.when(s == pl.num_programs(1) - 1)
    def _():
        qbd = _block_diag_queries(q_ref[...], PAIR_W, MB_DIM)
        sc = jnp.dot(qbd, km, precision=lax.Precision.HIGHEST, preferred_element_type=F32)
        picks = _top3_lanes(sc, n_blocks)
        lane_o = _iota((PAIR_W, PAIR_W), 1)
        out = jnp.zeros((PAIR_W, PAIR_W), jnp.int32)
        for k, pk in enumerate(picks):
            out = jnp.where(lane_o == k, pk, out)
        idx_ref[...] = out


def _mb_select(q, cache_k, page_table):
    nb = q.shape[0]
    n_pages = page_table.shape[1]
    steps = n_pages // _MEAN_PAGES
    n_blocks = n_pages * PAGE // MB_BLOCK
    assert n_pages % _MEAN_PAGES == 0 and MB_TOPK <= n_blocks <= PAIR_W
    page_specs = [pl.BlockSpec((None, HEAD_W, PAGE), lambda bi, s, pt, j=j: (pt[bi * n_pages + s * _MEAN_PAGES + j], 0, 0))
                  for j in range(_MEAN_PAGES)]
    return pl.pallas_call(
        functools.partial(_mb_select_kernel, n_blocks=n_blocks),
        grid_spec=pltpu.PrefetchScalarGridSpec(
            num_scalar_prefetch=1,
            grid=(nb, steps),
            in_specs=page_specs + [pl.BlockSpec((None, SAMPLE_PAD, HEAD_W), lambda bi, s, pt: (bi, 0, 0))],
            out_specs=pl.BlockSpec((None, PAIR_W, PAIR_W), lambda bi, s, pt: (bi, 0, 0)),
            scratch_shapes=[pltpu.VMEM((HEAD_W, PAIR_W), F32)],
        ),
        out_shape=jax.ShapeDtypeStruct((nb, PAIR_W, PAIR_W), jnp.int32),
        name="moba_select",
        compiler_params=_params(("arbitrary", "arbitrary")),
    )(page_table.reshape(-1), *([cache_k] * _MEAN_PAGES), q)


def _mb_sample_kernel(pt_ref, sel_ref, q_ref, kn_ref, vn_ref, z_ref, idx_ref, ck_hbm, cv_hbm, o_ref,
                      kbuf, vbuf, sem, kpad, vpad, s_sc, *, n_pages, t_real):
    b = pl.program_id(0)
    per_blk = MB_BLOCK // PAGE
    n_grp = MB_TOPK * t_real
    n_sel = n_grp * MB_BLOCK
    past = n_pages * PAGE

    def copies(req, slot, h, k, t, j):
        blk = sel_ref[(req * MB_HEADS * t_real + h * t_real + t) * MB_TOPK + k]
        phys = pt_ref[req * n_pages + blk * per_blk + j]
        hs = pl.ds(h * MB_DIM, MB_DIM)
        ls = pl.ds(((k * t_real + t) * per_blk + j) * PAGE, PAGE)
        return (pltpu.make_async_copy(ck_hbm.at[phys, hs, :], kbuf.at[slot, hs, ls], sem.at[0, slot]),
                pltpu.make_async_copy(cv_hbm.at[phys, hs, :], vbuf.at[slot, hs, ls], sem.at[1, slot]))

    todo = [(h, k, t, j) for h in range(MB_HEADS) for k in range(MB_TOPK) for t in range(t_real)
            for j in range(per_blk)]

    def fetch(req, slot):
        for a in todo:
            for cp in copies(req, slot, *a):
                cp.start()

    slot = b % 2

    @pl.when(b == 0)
    def _():
        fetch(0, 0)
        kpad[...] = jnp.zeros_like(kpad)
        vpad[...] = jnp.zeros_like(vpad)

    @pl.when(b + 1 < pl.num_programs(0))
    def _():
        fetch(b + 1, 1 - slot)

    kpad[0:SAMPLE_PAD, :] = kn_ref[...]
    vpad[0:SAMPLE_PAD, :] = vn_ref[...]

    nq = PAIR_W
    qbd = _block_diag_queries(q_ref[...] * (MB_DIM ** -0.5), nq, MB_DIM).astype(BF16)
    row = _iota((nq, 1), 0)
    t_row = row % SAMPLE_PAD
    slope = jnp.exp2(-(row // SAMPLE_PAD + 1).astype(F32))

    key = _iota((nq, PAGE), 1)
    s_new = _dot_nt(qbd, kpad[...]) - slope * (t_row - key).astype(F32)
    s_new = jnp.where((key <= t_row) & (key < t_real), s_new, NEG)
    s_sc[:, n_sel:n_sel + PAGE] = s_new
    m = jnp.max(s_new, axis=1, keepdims=True)

    for a in todo:
        for cp in copies(b, slot, *a):
            cp.wait()

    key_in_blk = _iota((nq, MB_BLOCK), 1)
    for g in range(n_grp):
        k, t = divmod(g, t_real)
        lanes = slice(g * MB_BLOCK, (g + 1) * MB_BLOCK)
        pos = idx_ref[:, k:k + 1] * MB_BLOCK + key_in_blk
        sg = _dot(qbd, kbuf[slot, :, lanes]) - slope * (past + t_row - pos).astype(F32)
        sg = jnp.where(t_row == t, sg, NEG)
        s_sc[:, lanes] = sg
        m = jnp.maximum(m, jnp.max(sg, axis=1, keepdims=True))

    pr = jnp.exp(s_sc[...] - m)
    num = _dot_nt(pr[:, :n_sel], vbuf[slot]) + _dot(pr[:, n_sel:], vpad[...])
    res = num / jnp.sum(pr, axis=1, keepdims=True)
    o_ref[...] = _gather_heads(res, MB_DIM) * _silu(z_ref[...])


def _mb_sample(q, kn, vn, z, idx, cache_k, cache_v, page_table, t_real):
    nb = q.shape[0]
    n_pages = page_table.shape[1]
    n_sel = MB_TOPK * t_real * MB_BLOCK
    sel = idx[:, :MB_HEADS * SAMPLE_PAD, :MB_TOPK].reshape(nb, MB_HEADS, SAMPLE_PAD, MB_TOPK)[:, :, :t_real]
    blk = pl.BlockSpec((None, SAMPLE_PAD, HEAD_W), lambda bi, pt, sl: (bi, 0, 0))
    hbm = pl.BlockSpec(memory_space=pl.ANY)
    return pl.pallas_call(
        functools.partial(_mb_sample_kernel, n_pages=n_pages, t_real=t_real),
        grid_spec=pltpu.PrefetchScalarGridSpec(
            num_scalar_prefetch=2,
            grid=(nb,),
            in_specs=[blk, blk, blk, blk,
                      pl.BlockSpec((None, PAIR_W, PAIR_W), lambda bi, pt, sl: (bi, 0, 0)),
                      hbm, hbm],
            out_specs=blk,
            scratch_shapes=[pltpu.VMEM((2, HEAD_W, n_sel), F32),
                            pltpu.VMEM((2, HEAD_W, n_sel), F32),
                            pltpu.SemaphoreType.DMA((2, 2)),
                            pltpu.VMEM((PAGE, HEAD_W), F32),
                            pltpu.VMEM((PAGE, HEAD_W), F32),
                            pltpu.VMEM((PAIR_W, n_sel + PAGE), F32)],
        ),
        out_shape=jax.ShapeDtypeStruct((nb, SAMPLE_PAD, HEAD_W), F32),
        name="moba_sample",
        compiler_params=_params(("arbitrary",), 48),
    )(page_table.reshape(-1), sel.reshape(-1), q, kn, vn, z, idx, cache_k, cache_v)


def _pad_cols(w, width):
    return jnp.pad(w, ((0, 0), (0, width - w.shape[1])))


def _pages_feature_major(cache):
    n_pool, page, heads, dim = cache.shape
    return jnp.transpose(cache, (0, 2, 3, 1)).reshape(n_pool, heads * dim, page)


def kernel(x_prompt, x_sample, cache_sb_k, cache_sb_v, cache_mb_k, cache_mb_v, state_ml_C, state_ml_n, state_ml_m, state_gla_S, page_table, c_prompt, c_sample, w_mod, b_mod, norm_g, final_g, w_in_even, w_out_even, ml_b_i, ml_b_f, ml_gn, w_in_odd, w_out_odd, gla_w_g2, gla_b_g, gla_gn):
    bsz, seq, _ = x_prompt.shape
    db, t_real, _ = x_sample.shape
    mp = bsz * seq
    ms = db * SAMPLE_PAD

    rows = -(-(bsz + db) // 8) * 8
    c_all = jnp.pad(jnp.concatenate([c_prompt, c_sample], axis=0), ((0, rows - bsz - db), (0, 0)))
    mod = _modulation(c_all, w_mod, b_mod)
    mod_p = mod[:, :bsz, None, :]
    mod_s = jnp.repeat(mod[:, bsz:bsz + db], SAMPLE_PAD, axis=1)[:, None]

    xp = x_prompt.reshape(mp, D_MODEL)
    xs = jnp.pad(x_sample, ((0, 0), (0, SAMPLE_PAD - t_real), (0, 0))).reshape(ms, D_MODEL)
    tm = 512

    w_even = jnp.concatenate([w_in_even[0, :, :9 * HEAD_W], _pad_cols(w_in_even[0, :, 9 * HEAD_W:], PAIR_W)],
                             axis=1).astype(BF16)
    widths_even = (HEAD_W,) * 9 + (PAIR_W,)
    gate_bias = _pad_cols(jnp.concatenate([ml_b_i[0], ml_b_f[0]])[None, :], PAIR_W)
    ml_gn0 = ml_gn[0].reshape(1, HEAD_W)
    w_out0 = w_out_even[0].astype(BF16)

    w_even_tok = jnp.concatenate([w_even[:, :HEAD_W], w_even[:, 3 * HEAD_W:]], axis=1)
    w_even_kv = w_even[:, HEAD_W:3 * HEAD_W].T
    sq, sz, mq, mk, mv, mo, mz, mg, sk, sv = _inproj(xp, mod_p[0], norm_g[0], w_even_tok, (HEAD_W,) * 7 + (PAIR_W,),
                                                     seq, tm, w_even_kv, (HEAD_W, HEAD_W))
    rp = lambda u: u.reshape(bsz, seq, HEAD_W)
    a_p = _sb_prompt(rp(sq), sk, sv, rp(sz)).reshape(mp, HEAD_W)
    zc = jnp.zeros((bsz, ML_HEADS, ML_DIM, ML_DIM), F32)
    zv = jnp.zeros((bsz, SAMPLE_PAD, ML_DIM), F32)
    b_p, mlc_p, mln_p, mlm_p = _mlstm(mq, mk, mv, mo, mz, mg, gate_bias, ml_gn0, zc, zv, zv, bsz, ML_CHUNK, ML_CHUNK)
    x1p = _outproj(a_p, b_p, xp, mod_p[0], w_out0, final_g, seq, tm, False)
    sbk_p, sbv_p = sk, sv

    sq, sk, sv, sz, mq, mk, mv, mo, mz, mg = _inproj(xs, mod_s[0], norm_g[0], w_even, widths_even, ms, ms)
    r3 = lambda u: u.reshape(db, SAMPLE_PAD, HEAD_W)
    a_s = _sb_sample(r3(sq), r3(sk), r3(sv), r3(sz), _pages_feature_major(cache_sb_k[0]),
                     _pages_feature_major(cache_sb_v[0]), page_table, t_real).reshape(ms, HEAD_W)
    n0 = jnp.pad(state_ml_n[0], ((0, 0), (0, SAMPLE_PAD - ML_HEADS), (0, 0)))
    m0 = jnp.broadcast_to(jnp.pad(state_ml_m[0], ((0, 0), (0, SAMPLE_PAD - ML_HEADS)))[:, :, None],
                          (db, SAMPLE_PAD, ML_DIM))
    b_s, mlc_s, mln_s, mlm_s = _mlstm(mq, mk, mv, mo, mz, mg, gate_bias, ml_gn0, state_ml_C[0], n0, m0,
                                      db, SAMPLE_PAD, t_real)
    x1s = _outproj(a_s, b_s, xs, mod_s[0], w_out0, final_g, ms, ms, False)
    sbk_s, sbv_s = r3(sk)[:, :t_real], r3(sv)[:, :t_real]

    kw = GLA_HEADS * GLA_DK
    g0 = 2 * kw + 2 * HEAD_W
    w_odd = jnp.concatenate([w_in_odd[0, :, :g0], w_in_odd[0, :, g0 + GLA_RANK:],
                             _pad_cols(w_in_odd[0, :, g0:g0 + GLA_RANK], PAIR_W)], axis=1).astype(BF16)
    widths_odd = (kw, kw) + (HEAD_W,) * 6 + (PAIR_W,)
    wg = jnp.pad(gla_w_g2[0], ((0, PAIR_W - GLA_RANK), (0, 0))).astype(BF16)
    bg = gla_b_g[0].reshape(1, kw)
    gla_gn0 = gla_gn[0].reshape(1, HEAD_W)
    w_out1 = w_out_odd[0].astype(BF16)

    c0 = 2 * kw + 3 * HEAD_W
    w_odd_tok = jnp.concatenate([w_odd[:, :c0], w_odd[:, c0 + 2 * HEAD_W:]], axis=1)
    w_odd_kv = w_odd[:, c0:c0 + 2 * HEAD_W].T
    gq, gk, gv, gz, dq, dz, gg, dk, dv = _inproj(x1p, mod_p[1], norm_g[1], w_odd_tok,
                                                 (kw, kw) + (HEAD_W,) * 4 + (PAIR_W,), seq, tm,
                                                 w_odd_kv, (HEAD_W, HEAD_W))
    s_zero = jnp.zeros((bsz, 2, PAIR_W, GLA_DV), F32)
    c_p, glas_p = _gla(gq, gk, gv, gz, gg, wg, bg, gla_gn0, s_zero, bsz, 256, GLA_SUB, 256)
    d_p = _moba_prompt(rp(dq), dk, dv, rp(dz)).reshape(mp, HEAD_W)
    y_p = _outproj(c_p, d_p, x1p, mod_p[1], w_out1, final_g, seq, tm, True)
    mbk_p, mbv_p = dk, dv

    gq, gk, gv, gz, dq, dk, dv, dz, gg = _inproj(x1s, mod_s[1], norm_g[1], w_odd, widths_odd, ms, ms)
    c_s, glas_s = _gla(gq, gk, gv, gz, gg, wg, bg, gla_gn0, state_gla_S[0].reshape(db, 2, PAIR_W, GLA_DV),
                       db, SAMPLE_PAD, SAMPLE_PAD, t_real)
    ck = _pages_feature_major(cache_mb_k[0])
    cv = _pages_feature_major(cache_mb_v[0])
    idx = _mb_select(r3(dq), ck, page_table)
    d_s = _mb_sample(r3(dq), r3(dk), r3(dv), r3(dz), idx, ck, cv, page_table, t_real).reshape(ms, HEAD_W)
    y_s = _outproj(c_s, d_s, x1s, mod_s[1], w_out1, final_g, ms, ms, True)
    mbk_s, mbv_s = r3(dk)[:, :t_real], r3(dv)[:, :t_real]

    kv_p = lambda u, h, d: jnp.transpose(u.reshape(1, bsz, h, d, seq), (0, 1, 4, 2, 3))
    kv_s = lambda u, h, d: u.reshape(1, db, t_real, h, d)
    return (y_p.reshape(bsz, seq, D_MODEL), y_s.reshape(db, SAMPLE_PAD, D_MODEL)[:, :t_real],
            kv_p(sbk_p, SB_HEADS, SB_DIM), kv_p(sbv_p, SB_HEADS, SB_DIM),
            kv_s(sbk_s, SB_HEADS, SB_DIM), kv_s(sbv_s, SB_HEADS, SB_DIM),
            mlc_p[None], mln_p[None, :, :ML_HEADS], mlm_p[None, :, :ML_HEADS, 0],
            mlc_s[None], mln_s[None, :, :ML_HEADS], mlm_s[None, :, :ML_HEADS, 0],
            glas_p.reshape(1, bsz, GLA_HEADS, GLA_DK, GLA_DV), glas_s.reshape(1, db, GLA_HEADS, GLA_DK, GLA_DV),
            kv_p(mbk_p, MB_HEADS, MB_DIM), kv_p(mbv_p, MB_HEADS, MB_DIM),
            kv_s(mbk_s, MB_HEADS, MB_DIM), kv_s(mbv_s, MB_HEADS, MB_DIM))
```

```python
import functools

import jax
import jax.numpy as jnp
from jax import lax
from jax.experimental import pallas as pl
from jax.experimental.pallas import tpu as pltpu

F32 = jnp.float32
BF16 = jnp.bfloat16

D_MODEL = 1024
EPS = 1e-6
PAGE = 128
SB_HEADS, SB_DIM = 8, 64
ML_HEADS, ML_DIM = 4, 128
ML_CHUNK = 256
GLA_HEADS, GLA_DK, GLA_DV, GLA_RANK, GLA_TAU = 4, 64, 128, 16, 16.0
MB_HEADS, MB_DIM, MB_BLOCK, MB_TOPK = 8, 64, 256, 3
HEAD_W = 512
PAIR_W = 128
SAMPLE_PAD = 8
NEG = -1e30
LOG2E = 1.4426950408889634
SB_SKIP = 88.0
SB_SKIP2 = SB_SKIP * 1.4426950408889634
GLA_BLOCK = 256
GLA_SUB = 16


def _dot(a, b):
    return jnp.dot(a.astype(BF16), b.astype(BF16), preferred_element_type=F32)


def _dot_nt(a, b):
    return lax.dot_general(a.astype(BF16), b.astype(BF16), (((1,), (1,)), ((), ())), preferred_element_type=F32)


def _dot_tn(a, b):
    return lax.dot_general(a.astype(BF16), b.astype(BF16), (((0,), (0,)), ((), ())), preferred_element_type=F32)


def _softplus_tail(z):
    return jnp.log(1.0 + jnp.exp(-jnp.abs(z)))


def _log_sigmoid(z):
    return jnp.minimum(z, 0.0) - _softplus_tail(z)


def _sigmoid(z):
    return 1.0 / (1.0 + jnp.exp(-z))


def _silu(z):
    return z * _sigmoid(z)


def _iota(shape, axis):
    return lax.broadcasted_iota(jnp.int32, shape, axis)


def _params(sem, vmem_mb=None):
    kw = dict(dimension_semantics=sem)
    if vmem_mb is not None:
        kw["vmem_limit_bytes"] = vmem_mb * 1024 * 1024
    return pltpu.CompilerParams(**kw)


def _mod_kernel(c_ref, w_ref, b_ref, o_ref):
    o_ref[...] = _dot(_silu(c_ref[...]), w_ref[...]) + b_ref[...]


def _modulation(c_all, w_mod, b_mod):
    depth = w_mod.shape[0]
    rows = c_all.shape[0]
    return pl.pallas_call(
        _mod_kernel,
        grid=(depth, 3),
        in_specs=[
            pl.BlockSpec((rows, D_MODEL), lambda l, j: (0, 0)),
            pl.BlockSpec((None, D_MODEL, D_MODEL), lambda l, j: (l, 0, j)),
            pl.BlockSpec((None, 1, D_MODEL), lambda l, j: (l, 0, j)),
        ],
        out_specs=pl.BlockSpec((None, rows, D_MODEL), lambda l, j: (l, 0, j)),
        out_shape=jax.ShapeDtypeStruct((depth, rows, 3 * D_MODEL), F32),
        name="modulation",
        compiler_params=_params(("arbitrary", "arbitrary")),
    )(c_all, w_mod, b_mod.reshape(depth, 1, 3 * D_MODEL))


def _inproj_kernel(x_ref, mod_ref, g_ref, w_ref, *refs, widths, widths_t):
    x = x_ref[...]
    y = x * lax.rsqrt(jnp.mean(x * x, axis=-1, keepdims=True) + EPS) * g_ref[...]
    mod = mod_ref[...]
    h = y * (1.0 + mod[:, D_MODEL:2 * D_MODEL]) + mod[:, :D_MODEL]
    hb = h.astype(BF16)
    out_refs = refs[1:] if widths_t else refs
    off = 0
    for o_ref, wd in zip(out_refs, widths):
        o_ref[...] = jnp.dot(hb, w_ref[:, off:off + wd], preferred_element_type=F32)
        off += wd
    if widths_t:
        wt_ref = refs[0]
        ht = h.T.astype(BF16)
        off = 0
        for o_ref, wd in zip(out_refs[len(widths):], widths_t):
            o_ref[...] = jnp.dot(wt_ref[off:off + wd, :], ht, preferred_element_type=F32)
            off += wd


def _inproj(x, mod, g, w, widths, rows_per_group, tm, wt=None, widths_t=()):
    m = x.shape[0]
    r = mod.shape[1]
    n = w.shape[1]
    groups = m // rows_per_group
    per_group = rows_per_group // tm
    in_specs = [
        pl.BlockSpec((tm, D_MODEL), lambda i: (i, 0)),
        pl.BlockSpec((None, r, 3 * D_MODEL), lambda i: ((i * tm) // rows_per_group, 0, 0)),
        pl.BlockSpec((1, D_MODEL), lambda i: (0, 0)),
        pl.BlockSpec((D_MODEL, n), lambda i: (0, 0), pipeline_mode=pl.Buffered(1)),
    ]
    args = [x, mod, g.reshape(1, D_MODEL), w]
    if widths_t:
        in_specs.append(pl.BlockSpec(wt.shape, lambda i: (0, 0), pipeline_mode=pl.Buffered(1)))
        args.append(wt)
    return pl.pallas_call(
        functools.partial(_inproj_kernel, widths=widths, widths_t=widths_t),
        grid=(m // tm,),
        in_specs=in_specs,
        out_specs=([pl.BlockSpec((tm, wd), lambda i: (i, 0)) for wd in widths]
                   + [pl.BlockSpec((None, wd, tm), lambda i: (i // per_group, 0, i % per_group)) for wd in widths_t]),
        out_shape=([jax.ShapeDtypeStruct((m, wd), F32) for wd in widths]
                   + [jax.ShapeDtypeStruct((groups, wd, rows_per_group), F32) for wd in widths_t]),
        name="inproj",
        compiler_params=_params(("arbitrary",), 56),
    )(*args)


def _outproj_kernel(a_ref, b_ref, x_ref, mod_ref, w_ref, fg_ref, o_ref, *, final):
    y = _dot(a_ref[...], w_ref[:HEAD_W, :]) + _dot(b_ref[...], w_ref[HEAD_W:, :])
    xn = x_ref[...] + mod_ref[...][:, 2 * D_MODEL:] * y
    if final:
        xn = xn * lax.rsqrt(jnp.mean(xn * xn, axis=-1, keepdims=True) + EPS) * fg_ref[...]
    o_ref[...] = xn


def _outproj(a, b, x, mod, w, final_g, rows_per_group, tm, final):
    m = x.shape[0]
    r = mod.shape[1]
    return pl.pallas_call(
        functools.partial(_outproj_kernel, final=final),
        grid=(m // tm,),
        in_specs=[
            pl.BlockSpec((tm, HEAD_W), lambda i: (i, 0)),
            pl.BlockSpec((tm, HEAD_W), lambda i: (i, 0)),
            pl.BlockSpec((tm, D_MODEL), lambda i: (i, 0)),
            pl.BlockSpec((None, r, 3 * D_MODEL), lambda i: ((i * tm) // rows_per_group, 0, 0)),
            pl.BlockSpec((2 * HEAD_W, D_MODEL), lambda i: (0, 0)),
            pl.BlockSpec((1, D_MODEL), lambda i: (0, 0)),
        ],
        out_specs=pl.BlockSpec((tm, D_MODEL), lambda i: (i, 0)),
        out_shape=jax.ShapeDtypeStruct((m, D_MODEL), F32),
        name="outproj",
        compiler_params=_params(("arbitrary",)),
    )(a, b, x, mod, w, final_g.reshape(1, D_MODEL))


SB_GROUP = 4


def _later_matrix(n):
    return jnp.where(_iota((n, n), 0) > _iota((n, n), 1), 1.0, 0.0).astype(BF16)


def _sb_rows_tile(z2, valid, carry, later):
    sp = jnp.maximum(z2, 0.0) + jnp.log2(1.0 + jnp.exp2(-jnp.abs(z2)))
    if valid is not None:
        sp = jnp.where(valid, sp, 0.0)
    hi = sp.astype(BF16)
    lo = (sp - hi.astype(F32)).astype(BF16)
    t = sp + jnp.dot(hi, later, preferred_element_type=F32) + jnp.dot(lo, later, preferred_element_type=F32) + carry
    w = jnp.exp2(z2 - t)
    if valid is not None:
        w = jnp.where(valid, w, 0.0)
    return w, t[:, 0:1]


def _sb_prompt_kernel(q_ref, z_ref, kt_hbm, vt_hbm, o_ref, kt_sc, vt_sc, sem, acc_sc,
                      qst_sc, z_sc, sp_sc, hilo_sc, carry_sc, *, tq):
    b = pl.program_id(0)
    i = pl.program_id(1)

    @pl.when(i == 0)
    def _():
        ck = pltpu.make_async_copy(kt_hbm.at[b], kt_sc, sem.at[0])
        cv = pltpu.make_async_copy(vt_hbm.at[b], vt_sc, sem.at[1])
        ck.start()
        cv.start()
        ck.wait()
        cv.wait()

    n_pairs = SB_HEADS // 2
    rows_all = SB_HEADS * tq
    later = _later_matrix(tq)
    lane_head = _iota((tq, PAIR_W), 1) // SB_DIM
    row_head = _iota((PAIR_W, tq), 0) // SB_DIM
    q = q_ref[...] * (SB_DIM ** -0.5 * LOG2E)
    for p in range(n_pairs):
        qp = q[:, p * PAIR_W:(p + 1) * PAIR_W]
        for hh in range(2):
            qst_sc[p, hh * tq:(hh + 1) * tq, :] = jnp.where(lane_head == hh, qp, 0.0).astype(BF16)
    strictly_before = _iota((2 * tq, tq), 1) < _iota((2 * tq, tq), 0) % tq

    def tile(j, diag):
        start = pl.multiple_of(j * tq, tq)
        for g in range(n_pairs // SB_GROUP):
            pairs = range(g * SB_GROUP, (g + 1) * SB_GROUP)
            for p in pairs:
                rs = slice(p * 2 * tq, (p + 1) * 2 * tq)
                ktp = kt_sc[p * PAIR_W:(p + 1) * PAIR_W, pl.ds(start, tq)].astype(BF16)
                z2 = jnp.dot(qst_sc[p], ktp, preferred_element_type=F32)
                sp = jnp.maximum(z2, 0.0) + jnp.log2(1.0 + jnp.exp2(-jnp.abs(z2)))
                if diag:
                    sp = jnp.where(strictly_before, sp, 0.0)
                hi = sp.astype(BF16)
                z_sc[rs, :] = z2
                sp_sc[rs, :] = sp
                hilo_sc[rs, :] = hi
                hilo_sc[p * 2 * tq + rows_all:(p + 1) * 2 * tq + rows_all, :] = (sp - hi.astype(F32)).astype(BF16)
            g0 = g * SB_GROUP * 2 * tq
            g1 = (g + 1) * SB_GROUP * 2 * tq
            cum_hi = jnp.dot(hilo_sc[g0:g1, :], later, preferred_element_type=F32)
            cum_lo = jnp.dot(hilo_sc[rows_all + g0:rows_all + g1, :], later, preferred_element_type=F32)
            for p in pairs:
                rs = slice(p * 2 * tq, (p + 1) * 2 * tq)
                rg = slice(p * 2 * tq - g0, (p + 1) * 2 * tq - g0)
                t = sp_sc[rs, :] + cum_hi[rg] + cum_lo[rg]
                if not diag:
                    t = t + carry_sc[rs, :]
                w = jnp.exp2((z_sc[rs, :] - t).astype(BF16))
                if diag:
                    w = jnp.where(strictly_before, w, jnp.zeros_like(w))
                carry_sc[rs, :] = t[:, 0:1]
                vtp = vt_sc[p * PAIR_W:(p + 1) * PAIR_W, pl.ds(start, tq)]
                w2 = jnp.concatenate([w[0:tq], w[tq:2 * tq]], axis=1)
                v2 = jnp.concatenate([jnp.where(row_head == hh, vtp, 0.0) for hh in range(2)], axis=1)
                upd = _dot_nt(w2, v2)
                if diag:
                    acc_sc[:, p * PAIR_W:(p + 1) * PAIR_W] = upd
                else:
                    acc_sc[:, p * PAIR_W:(p + 1) * PAIR_W] += upd
        return jnp.min(carry_sc[...])

    least = tile(i, True)

    def cond(c):
        return (c[0] >= 0) & (c[1] < SB_SKIP2)

    def body(c):
        return c[0] - 1, tile(c[0], False)

    lax.while_loop(cond, body, (i - 1, least))
    o_ref[...] = acc_sc[...] * _silu(z_ref[...])


def _sb_prompt(q, kt, vt, z, tq=128):
    b, s, _ = q.shape
    blk = pl.BlockSpec((None, tq, HEAD_W), lambda bi, i: (bi, i, 0))
    hbm = pl.BlockSpec(memory_space=pl.ANY)
    return pl.pallas_call(
        functools.partial(_sb_prompt_kernel, tq=tq),
        grid=(b, s // tq),
        in_specs=[blk, blk, hbm, hbm],
        out_specs=blk,
        out_shape=jax.ShapeDtypeStruct((b, s, HEAD_W), F32),
        scratch_shapes=[pltpu.VMEM((HEAD_W, s), F32),
                        pltpu.VMEM((HEAD_W, s), F32),
                        pltpu.SemaphoreType.DMA((2,)),
                        pltpu.VMEM((tq, HEAD_W), F32),
                        pltpu.VMEM((SB_HEADS // 2, 2 * tq, PAIR_W), BF16),
                        pltpu.VMEM((SB_HEADS * tq, tq), F32),
                        pltpu.VMEM((SB_HEADS * tq, tq), F32),
                        pltpu.VMEM((2 * SB_HEADS * tq, tq), BF16),
                        pltpu.VMEM((SB_HEADS * tq, 1), F32)],
        name="sb_prompt",
        compiler_params=_params(("arbitrary", "arbitrary"), 48),
    )(q, z, kt, vt)


def _tile_rows(x, reps):
    return jnp.concatenate([x] * reps, axis=0)


def _block_diag_queries(q, n_cols, head_dim):
    qt = _tile_rows(q, n_cols // SAMPLE_PAD)
    row_head = _iota((n_cols, HEAD_W), 0) // SAMPLE_PAD
    lane_head = _iota((n_cols, HEAD_W), 1) // head_dim
    return jnp.where(row_head == lane_head, qt, 0.0)


def _gather_heads(res, head_dim):
    lane_head = _iota((SAMPLE_PAD, HEAD_W), 1) // head_dim
    out = jnp.zeros((SAMPLE_PAD, HEAD_W), F32)
    for h in range(HEAD_W // head_dim):
        out = out + jnp.where(lane_head == h, res[h * SAMPLE_PAD:(h + 1) * SAMPLE_PAD, :], 0.0)
    return out


def _sb_sample_kernel(pt_ref, q_ref, kn_ref, vn_ref, z_ref, ck_hbm, cv_hbm, o_ref,
                      kbuf, vbuf, sem, acc_ref, kpad, vpad, *, n_pages, t_real):
    b = pl.program_id(0)
    nq = PAIR_W

    def copies(p, slot, req=b):
        phys = pt_ref[req * n_pages + p]
        return (pltpu.make_async_copy(ck_hbm.at[phys], kbuf.at[slot], sem.at[0, slot]),
                pltpu.make_async_copy(cv_hbm.at[phys], vbuf.at[slot], sem.at[1, slot]))

    def start(p, slot, req=b):
        for cp in copies(p, slot, req):
            cp.start()

    def wait(p, slot):
        for cp in copies(p, slot):
            cp.wait()

    def first_pages(req):
        start(n_pages - 1, (n_pages - 1) % 2, req)
        start(n_pages - 2, (n_pages - 2) % 2, req)

    @pl.when(b == 0)
    def _():
        first_pages(0)
        kpad[...] = jnp.zeros_like(kpad)
        vpad[...] = jnp.zeros_like(vpad)

    kpad[0:SAMPLE_PAD, :] = kn_ref[...]
    vpad[0:SAMPLE_PAD, :] = vn_ref[...]

    qbd = _block_diag_queries(q_ref[...] * (SB_DIM ** -0.5 * LOG2E), nq, SB_DIM).astype(BF16)
    later = _later_matrix(PAGE)
    qrow = _iota((nq, PAGE), 0)
    key = _iota((nq, PAGE), 1)
    valid_new = (key < qrow % SAMPLE_PAD) & (key < t_real)
    row1 = _iota((nq, 1), 0)
    real_row = (row1 < SB_HEADS * SAMPLE_PAD) & (row1 % SAMPLE_PAD < t_real)

    w, carry = _sb_rows_tile(_dot_nt(qbd, kpad[...]), valid_new, jnp.zeros((nq, 1), F32), later)
    acc_ref[...] = _dot(w, vpad[...])

    def least(c):
        return jnp.min(jnp.where(real_row, c, jnp.inf))

    def cond(c):
        return (c[0] >= 0) & (c[2] < SB_SKIP2)

    def body(c):
        p, carry, _ = c
        slot = p % 2
        wait(p, slot)
        w, cn = _sb_rows_tile(jnp.dot(qbd, kbuf[slot].astype(BF16), preferred_element_type=F32), None, carry, later)
        acc_ref[...] += _dot_nt(w, vbuf[slot])

        @pl.when(p >= 2)
        def _():
            start(p - 2, slot)

        return p - 1, cn, least(cn)

    p_end, _, _ = lax.while_loop(cond, body, (jnp.int32(n_pages - 1), carry, least(carry)))

    @pl.when(p_end >= 0)
    def _():
        wait(p_end, p_end % 2)

    @pl.when(p_end >= 1)
    def _():
        wait(p_end - 1, (p_end - 1) % 2)

    @pl.when(b + 1 < pl.num_programs(0))
    def _():
        first_pages(b + 1)

    o_ref[...] = _gather_heads(acc_ref[...], SB_DIM) * _silu(z_ref[...])


def _sb_sample(q, kn, vn, z, cache_k, cache_v, page_table, t_real):
    nb = q.shape[0]
    n_pages = page_table.shape[1]
    blk = pl.BlockSpec((None, SAMPLE_PAD, HEAD_W), lambda bi, pt: (bi, 0, 0))
    hbm = pl.BlockSpec(memory_space=pl.ANY)
    return pl.pallas_call(
        functools.partial(_sb_sample_kernel, n_pages=n_pages, t_real=t_real),
        grid_spec=pltpu.PrefetchScalarGridSpec(
            num_scalar_prefetch=1,
            grid=(nb,),
            in_specs=[blk, blk, blk, blk, hbm, hbm],
            out_specs=blk,
            scratch_shapes=[
                pltpu.VMEM((2, HEAD_W, PAGE), F32),
                pltpu.VMEM((2, HEAD_W, PAGE), F32),
                pltpu.SemaphoreType.DMA((2, 2)),
                pltpu.VMEM((PAIR_W, HEAD_W), F32),
                pltpu.VMEM((PAGE, HEAD_W), F32),
                pltpu.VMEM((PAGE, HEAD_W), F32),
            ],
        ),
        out_shape=jax.ShapeDtypeStruct((nb, SAMPLE_PAD, HEAD_W), F32),
        name="sb_sample",
        compiler_params=_params(("arbitrary",)),
    )(page_table.reshape(-1), q, kn, vn, z, cache_k, cache_v)


def _mlstm_kernel(q_ref, k_ref, v_ref, og_ref, z_ref, gate_ref, bias_ref, gn_ref, c0_ref, n0_ref, m0_ref,
                  out_ref, c_out, n_out, m_out, c_sc, n_sc, m_sc, *, chunk, valid, per_step):
    ci = pl.program_id(1)

    @pl.when(ci == 0)
    def _():
        c_sc[...] = c0_ref[...]
        n_sc[...] = n0_ref[...]
        m_sc[...] = m0_ref[...]

    for bb in range(per_step):
        _mlstm_chunk(q_ref.at[bb], k_ref.at[bb], v_ref.at[bb], og_ref.at[bb], z_ref.at[bb], gate_ref.at[bb],
                     bias_ref, gn_ref, out_ref.at[bb], c_sc.at[bb], n_sc.at[bb], m_sc.at[bb], chunk, valid)

    @pl.when(ci == pl.num_programs(1) - 1)
    def _():
        c_out[...] = c_sc[...]
        n_out[...] = n_sc[...]
        m_out[...] = m_sc[...]


def _mlstm_chunk(q_ref, k_ref, v_ref, og_ref, z_ref, gate_ref, bias_ref, gn_ref, out_ref, c_sc, n_sc, m_sc,
                 chunk, valid):
    pre = gate_ref[...] + bias_ref[...]
    lf = pltpu.roll(_log_sigmoid(pre), PAIR_W - ML_HEADS, 1)
    ig = pre
    if valid < chunk:
        live = _iota((chunk, PAIR_W), 0) < valid
        lf = jnp.where(live, lf, 0.0)
        ig = jnp.where(live, ig, -jnp.inf)
    r = _iota((chunk, chunk), 0)
    c = _iota((chunk, chunk), 1)
    tril = c <= r
    lf_hi = lf.astype(BF16)
    bc2 = jnp.dot(jnp.where(tril, 1.0, 0.0).astype(BF16),
                  jnp.concatenate([lf_hi, (lf - lf_hi.astype(F32)).astype(BF16)], axis=1), preferred_element_type=F32)
    bc = bc2[:, :PAIR_W] + bc2[:, PAIR_W:]
    a = ig - bc
    a_t = a.T
    q_all = q_ref[...]
    k_all = k_ref[...] * (ML_DIM ** -0.5)
    v_all = v_ref[...]
    rows_st = ML_HEADS * chunk
    q_st = jnp.where(_iota((rows_st, HEAD_W), 0) // chunk == _iota((rows_st, HEAD_W), 1) // ML_DIM,
                     jnp.concatenate([q_all] * ML_HEADS, axis=0), 0.0).astype(BF16)
    qk_st = _dot_nt(q_st, k_all)
    qc_st = _dot(q_st, c_sc[...].reshape(ML_HEADS * ML_DIM, ML_DIM))
    wts, ms, inters, wks, decays = [], [], [], [], []
    for h in range(ML_HEADS):
        sl = slice(h * ML_DIM, (h + 1) * ML_DIM)
        bc_h = bc[:, h:h + 1]
        a_row = a_t[h:h + 1, :]
        m0 = m_sc[h:h + 1, 0:1]
        cm = jnp.max(jnp.where(tril, a_row, -jnp.inf), axis=-1, keepdims=True)
        m_h = jnp.maximum(bc_h + m0, cm + bc_h)
        dmat = jnp.exp(jnp.where(tril, bc_h - m_h + a_row, -jnp.inf))
        wts.append(dmat * qk_st[h * chunk:(h + 1) * chunk])
        ms.append(m_h)
        inters.append(jnp.exp(bc_h + m0 - m_h))
        bc_l = bc_h[chunk - 1:chunk, :]
        m_l = m_h[chunk - 1:chunk, :]
        decays.append(jnp.exp(bc_l + m0 - m_l))
        wks.append(jnp.exp(bc_l + a[:, h:h + 1] - m_l) * k_all[:, sl])
        m_sc[h:h + 1, :] = jnp.broadcast_to(m_l, (1, PAIR_W))
    wv_st = _dot(jnp.concatenate(wts, axis=0), v_all)
    wk_all = jnp.concatenate(wks, axis=1)
    upd = _dot_tn(wk_all, v_all)
    for h in range(ML_HEADS):
        sl = slice(h * ML_DIM, (h + 1) * ML_DIM)
        rs = slice(h * chunk, (h + 1) * chunk)
        n_h = n_sc[h:h + 1, :]
        num = inters[h] * qc_st[rs] + wv_st[rs, sl]
        den = (inters[h] * jnp.sum(q_all[:, sl] * n_h, axis=-1, keepdims=True)
               + jnp.sum(wts[h], axis=-1, keepdims=True))
        hv = num / jnp.maximum(jnp.abs(den), jnp.exp(-ms[h]))
        hn = hv * lax.rsqrt(jnp.mean(hv * hv, axis=-1, keepdims=True) + EPS) * gn_ref[:, sl]
        out_ref[:, sl] = hn * _sigmoid(og_ref[:, sl]) * _silu(z_ref[:, sl])
        c_sc[h] = decays[h] * c_sc[h] + upd[sl, sl]
        n_sc[h:h + 1, :] = decays[h] * n_h + jnp.sum(wks[h], axis=0, keepdims=True)


ML_PER_STEP = 1


def _mlstm(q, k, v, og, z, gates, bias, gn, c0, n0, m0, nb, chunk, valid):
    m = q.shape[0]
    t = m // nb
    n_chunks = t // chunk
    per = ML_PER_STEP
    tok = pl.BlockSpec((per, chunk, HEAD_W), lambda g, ci: (g, ci, 0))
    st_c = pl.BlockSpec((per, ML_HEADS, ML_DIM, ML_DIM), lambda g, ci: (g, 0, 0, 0))
    st_v = pl.BlockSpec((per, SAMPLE_PAD, ML_DIM), lambda g, ci: (g, 0, 0))
    r3 = lambda u: u.reshape(nb, t, u.shape[-1])
    out, c1, n1, m1 = pl.pallas_call(
        functools.partial(_mlstm_kernel, chunk=chunk, valid=valid, per_step=per),
        grid=(nb // per, n_chunks),
        in_specs=[tok, tok, tok, tok, tok,
                  pl.BlockSpec((per, chunk, PAIR_W), lambda g, ci: (g, ci, 0)),
                  pl.BlockSpec((1, PAIR_W), lambda g, ci: (0, 0)),
                  pl.BlockSpec((1, HEAD_W), lambda g, ci: (0, 0)),
                  st_c, st_v, st_v],
        out_specs=[tok, st_c, st_v, st_v],
        out_shape=[jax.ShapeDtypeStruct((nb, t, HEAD_W), F32),
                   jax.ShapeDtypeStruct((nb, ML_HEADS, ML_DIM, ML_DIM), F32),
                   jax.ShapeDtypeStruct((nb, SAMPLE_PAD, ML_DIM), F32),
                   jax.ShapeDtypeStruct((nb, SAMPLE_PAD, ML_DIM), F32)],
        scratch_shapes=[pltpu.VMEM((per, ML_HEADS, ML_DIM, ML_DIM), F32),
                        pltpu.VMEM((per, SAMPLE_PAD, ML_DIM), F32),
                        pltpu.VMEM((per, SAMPLE_PAD, ML_DIM), F32)],
        name="mlstm",
        compiler_params=_params(("arbitrary", "arbitrary")),
    )(r3(q), r3(k), r3(v), r3(og), r3(z), r3(gates), bias, gn, c0, n0, m0)
    return out.reshape(m, HEAD_W), c1, n1, m1


def _gla_kernel(q_ref, k_ref, v_ref, z_ref, gg_ref, wg_ref, bg_ref, gn_ref, s0_ref, out_ref, s_out,
                st_sc, kb, gb, vb, qa_sc, kc_sc, dec_sc, oi_sc, *, tb, sub, valid):
    ti = pl.program_id(1)
    kw = GLA_HEADS * GLA_DK

    @pl.when(ti == 0)
    def _():
        for p in range(2):
            st_sc[p] = s0_ref[p].T
        kb[0:sub, :] = jnp.zeros((sub, kw), F32)
        gb[0:sub, :] = jnp.zeros((sub, kw), F32)
        vb[0:sub, :] = jnp.zeros((sub, HEAD_W), F32)

    lg = _log_sigmoid(_dot(gg_ref[...], wg_ref[...]) + bg_ref[...]) * (1.0 / GLA_TAU)
    k = k_ref[...]
    if valid < tb:
        live = _iota((tb, kw), 0) < valid
        lg = jnp.where(live, lg, 0.0)
        k = jnp.where(live, k, 0.0)
    r = _iota((tb, tb), 0)
    c = _iota((tb, tb), 1)
    same = (r // sub) == (c // sub)
    sums = jnp.concatenate([jnp.where(same & (c <= r), 1.0, 0.0), jnp.where(same, 1.0, 0.0)], axis=0).astype(BF16)
    lg_hi = lg.astype(BF16)
    parts = jnp.dot(sums, jnp.concatenate([lg_hi, (lg - lg_hi.astype(F32)).astype(BF16)], axis=1),
                    preferred_element_type=F32)
    g = parts[:tb, :kw] + parts[:tb, kw:]
    gend = parts[tb:, :kw] + parts[tb:, kw:]
    q = q_ref[...] * (GLA_DK ** -0.5)
    v = v_ref[...]
    qa_sc[...] = q * jnp.exp(g)
    kc_sc[...] = k * jnp.exp(gend - g)
    dec_sc[...] = jnp.exp(gend)
    kb[sub:sub + tb, :] = k
    gb[sub:sub + tb, :] = g
    vb[sub:sub + tb, :] = v

    head_spread = jnp.where(_iota((kw, HEAD_W), 0) // GLA_DK == _iota((kw, HEAD_W), 1) // GLA_DV, 1.0, 0.0).astype(BF16)
    rmod = _iota((tb, 1), 0) % sub
    oi = jnp.zeros((tb, HEAD_W), F32)
    for d in range(sub):
        lo = sub - d
        dec = jnp.exp(jnp.where(rmod >= d, g - gb[lo:lo + tb, :], NEG))
        oi = oi + _dot(q * kb[lo:lo + tb, :] * dec, head_spread) * vb[lo:lo + tb, :]
    oi_sc[...] = oi

    lane_head = _iota((2 * sub, PAIR_W), 1) // GLA_DK
    row_head = _iota((2 * sub, PAIR_W), 0) // sub
    for i in range(tb // sub):
        rows = slice(i * sub, (i + 1) * sub)
        qa = qa_sc[rows, :]
        kc = kc_sc[rows, :]
        dec = dec_sc[rows, :][0:1, :]
        for p in range(2):
            ps = slice(p * PAIR_W, (p + 1) * PAIR_W)
            vs = slice(2 * p * GLA_DV, (2 * p + 2) * GLA_DV)
            st = st_sc[p]
            own = lane_head == row_head
            o2 = _dot_nt(jnp.where(own, jnp.concatenate([qa[:, ps]] * 2, axis=0), 0.0), st)
            v2 = jnp.concatenate([v_ref[rows, vs][:, :GLA_DV], v_ref[rows, vs][:, GLA_DV:]], axis=0)
            st_sc[p] = st * dec[:, ps] + _dot_tn(v2, jnp.where(own, jnp.concatenate([kc[:, ps]] * 2, axis=0), 0.0))
            for hh in range(2):
                hs = slice((2 * p + hh) * GLA_DV, (2 * p + hh + 1) * GLA_DV)
                o = o2[hh * sub:(hh + 1) * sub] + oi_sc[rows, hs]
                on = o * lax.rsqrt(jnp.mean(o * o, axis=-1, keepdims=True) + EPS) * gn_ref[:, hs]
                out_ref[rows, hs] = on * _silu(z_ref[rows, hs])

    @pl.when(ti == pl.num_programs(1) - 1)
    def _():
        for p in range(2):
            s_out[p] = st_sc[p].T


def _gla(q, k, v, z, gg, wg, bg, gn, s0, nb, tb, sub, valid):
    m = q.shape[0]
    nt = m // (nb * tb)
    kw = GLA_HEADS * GLA_DK
    tok_k = pl.BlockSpec((tb, kw), lambda b, t: (b * nt + t, 0))
    tok_v = pl.BlockSpec((tb, HEAD_W), lambda b, t: (b * nt + t, 0))
    st = pl.BlockSpec((None, 2, PAIR_W, GLA_DV), lambda b, t: (b, 0, 0, 0))
    return pl.pallas_call(
        functools.partial(_gla_kernel, tb=tb, sub=sub, valid=valid),
        grid=(nb, nt),
        in_specs=[tok_k, tok_k, tok_v, tok_v,
                  pl.BlockSpec((tb, PAIR_W), lambda b, t: (b * nt + t, 0)),
                  pl.BlockSpec((PAIR_W, kw), lambda b, t: (0, 0)),
                  pl.BlockSpec((1, kw), lambda b, t: (0, 0)),
                  pl.BlockSpec((1, HEAD_W), lambda b, t: (0, 0)),
                  st],
        out_specs=[tok_v, st],
        out_shape=[jax.ShapeDtypeStruct((m, HEAD_W), F32),
                   jax.ShapeDtypeStruct((nb, 2, PAIR_W, GLA_DV), F32)],
        scratch_shapes=[pltpu.VMEM((2, GLA_DV, PAIR_W), F32),
                        pltpu.VMEM((sub + tb, kw), F32),
                        pltpu.VMEM((sub + tb, kw), F32),
                        pltpu.VMEM((sub + tb, HEAD_W), F32),
                        pltpu.VMEM((tb, kw), F32),
                        pltpu.VMEM((tb, kw), F32),
                        pltpu.VMEM((tb, kw), F32),
                        pltpu.VMEM((tb, HEAD_W), F32)],
        name="gla",
        compiler_params=_params(("arbitrary", "arbitrary")),
    )(q, k, v, z, gg, wg, bg, gn, s0)


def _top3_rows(s, n_valid):
    nb = s.shape[0]
    nrow = _iota(s.shape, 0)
    live = nrow < n_valid
    s = jnp.where(live, s, -jnp.inf)
    chosen = jnp.zeros(s.shape, F32)
    picks = []
    for _ in range(MB_TOPK):
        mx = jnp.max(s, axis=0, keepdims=True)
        idx = jnp.min(jnp.where(s == mx, nrow, nb), axis=0, keepdims=True)
        pick = nrow == idx
        chosen = jnp.where(pick, jnp.where(live, 1.0, 0.0), chosen)
        s = jnp.where(pick, -jnp.inf, s)
        picks.append(idx)
    return picks, chosen


KEY_CHUNK = 64
MB_TRIP = 4
SUM_ROWS = 16


def _moba_prompt_kernel(q_ref, kt_ref, vt_ref, z_ref, o_ref,
                        ktok_sc, vtbd_sc, km_sc, bias_sc, sel2_sc, sel_sc, qtm_sc, p_sc, acc_sc, m_sc, *score_bufs,
                        tq, nblk):
    s_bufs, cm_bufs = score_bufs[:MB_TRIP + 1], score_bufs[MB_TRIP + 1:]
    p = pl.program_id(1)
    i = pl.program_id(2)
    slope2 = [jnp.exp2(-(jnp.full((1, tq), 2 * p + hh + 1, jnp.int32)).astype(F32)) * LOG2E for hh in range(2)]

    row_head = _iota((PAIR_W, tq), 0) // MB_DIM

    @pl.when(i == 0)
    def _():
        lane = _iota((PAIR_W, PAIR_W), 1)
        feat_head = _iota((PAIR_W, MB_BLOCK), 0) // MB_DIM
        km = jnp.zeros((PAIR_W, PAIR_W), F32)
        for n in range(nblk):
            cols = slice(n * MB_BLOCK, (n + 1) * MB_BLOCK)
            blk = kt_ref[:, cols]
            ktok_sc[cols, :] = blk.T.astype(BF16)
            km = jnp.where(lane == n, jnp.mean(blk, axis=1, keepdims=True), km)
            vblk = vt_ref[:, cols]
            ones_row = _iota((SUM_ROWS, MB_BLOCK), 0)
            for hh in range(2):
                vtbd_sc[0:PAIR_W, (2 * n + hh) * MB_BLOCK:(2 * n + hh + 1) * MB_BLOCK] = (
                    jnp.where(feat_head == hh, vblk, 0.0).astype(BF16))
                vtbd_sc[PAIR_W:PAIR_W + SUM_ROWS, (2 * n + hh) * MB_BLOCK:(2 * n + hh + 1) * MB_BLOCK] = (
                    jnp.where(ones_row == hh, 1.0, 0.0).astype(BF16))
        km_t = km.T
        blk_head = _iota((PAIR_W, PAIR_W), 1) // MB_DIM
        for hh in range(2):
            km_sc[hh * nblk:(hh + 1) * nblk, :] = jnp.where(blk_head == hh, km_t, 0.0)[0:nblk]
        key_off = _iota((MB_BLOCK, tq), 0).astype(F32)
        for hh in range(2):
            bias_sc[hh] = slope2[hh] * key_off

    q_t = q_ref[...].T
    s_sel = jnp.dot(km_sc[...], q_t, precision=lax.Precision.HIGHEST, preferred_element_type=F32)
    for hh in range(2):
        _, chosen = _top3_rows(s_sel[hh * nblk:(hh + 1) * nblk], i)
        sel2_sc[...] = chosen
        for n in range(nblk):
            sel_sc[hh, n] = sel2_sc[n:n + 1, :]
        qtm_sc[:, hh * tq:(hh + 1) * tq] = jnp.where(row_head == hh, q_t * (MB_DIM ** -0.5 * LOG2E), 0.0).astype(BF16)

    n_chunks = MB_BLOCK // KEY_CHUNK

    bufs = tuple((s_bufs[j], cm_bufs[j]) for j in range(MB_TRIP + 1))
    acc_row = _iota((PAIR_W + SUM_ROWS, tq), 0)
    acc_head0 = (acc_row < MB_DIM) | (acc_row == PAIR_W)

    def scores(n, buf, diag):
        s_ref, cm_ref = bufs[buf]
        start = pl.multiple_of(n * MB_BLOCK, MB_BLOCK)
        s = jnp.dot(ktok_sc[pl.ds(start, MB_BLOCK), :], qtm_sc[...], preferred_element_type=F32)
        for hh in range(2):
            cm = None
            for kc in range(n_chunks):
                rows = slice(kc * KEY_CHUNK, (kc + 1) * KEY_CHUNK)
                sc = s[rows, hh * tq:(hh + 1) * tq] + bias_sc[hh, rows]
                if diag:
                    key = _iota((KEY_CHUNK, tq), 0) + kc * KEY_CHUNK
                    sc = jnp.where(key <= _iota((KEY_CHUNK, tq), 1), sc, NEG)
                s_ref[hh, rows] = sc
                c = jnp.max(sc, axis=0, keepdims=True)
                cm = c if cm is None else jnp.maximum(cm, c)
            cm_ref[hh] = cm

    def attend(n, buf, diag):
        s_ref, cm_ref = bufs[buf]
        alphas = []
        for hh in range(2):
            cm = cm_ref[hh]
            if diag:
                m_new = cm
                shift = cm
            else:
                c = slope2[hh] * ((n - i) * MB_BLOCK).astype(F32)
                ok = sel_sc[hh, n] > 0.5
                m_old = m_sc[hh]
                m_new = jnp.maximum(m_old, jnp.where(ok, cm + c, NEG))
                shift = jnp.where(ok, m_new - c, -NEG)
                alphas.append(jnp.exp2(m_old - m_new))
            for kc in range(n_chunks):
                rows = slice(kc * KEY_CHUNK, (kc + 1) * KEY_CHUNK)
                p_sc[hh * MB_BLOCK + kc * KEY_CHUNK:hh * MB_BLOCK + (kc + 1) * KEY_CHUNK, :] = (
                    jnp.exp2((s_ref[hh, rows] - shift).astype(BF16)))
            m_sc[hh] = m_new
        vt = vtbd_sc[:, pl.ds(pl.multiple_of(n * (2 * MB_BLOCK), 2 * MB_BLOCK), 2 * MB_BLOCK)]
        pv = jnp.dot(vt, p_sc[...], preferred_element_type=F32)
        if diag:
            acc_sc[...] = pv
        else:
            acc_sc[...] = acc_sc[...] * jnp.where(acc_head0, alphas[0], alphas[1]) + pv

    last = nblk - 1
    scores(i, MB_TRIP, True)
    for j in range(MB_TRIP):
        scores(min(j, last), j, False)
    attend(i, MB_TRIP, True)

    def body(k, _):
        n0 = MB_TRIP * k
        for j in range(MB_TRIP):
            attend(n0 + j, j, False)
            scores(jnp.minimum(n0 + j + MB_TRIP, last), j, False)
        return 0

    full = i // MB_TRIP
    lax.fori_loop(0, full, body, 0)
    for j in range(MB_TRIP - 1):
        @pl.when(full * MB_TRIP + j < i)
        def _(j=j):
            attend(full * MB_TRIP + j, j, False)
    inv = [1.0 / acc_sc[PAIR_W + hh:PAIR_W + hh + 1, :] for hh in range(2)]
    out_t = acc_sc[0:PAIR_W, :] * jnp.where(row_head == 0, inv[0], inv[1])
    o_ref[...] = out_t.T * _silu(z_ref[...])


def _moba_prompt(q, kt, vt, z):
    b, s, _ = q.shape
    tq = MB_BLOCK
    nblk = s // MB_BLOCK
    assert nblk <= PAIR_W
    blk_q = pl.BlockSpec((None, tq, PAIR_W), lambda bi, p, i: (bi, i, p))
    blk_kv = pl.BlockSpec((None, PAIR_W, s), lambda bi, p, i: (bi, p, 0))
    return pl.pallas_call(
        functools.partial(_moba_prompt_kernel, tq=tq, nblk=nblk),
        grid=(b, HEAD_W // PAIR_W, s // tq),
        in_specs=[blk_q, blk_kv, blk_kv, blk_q],
        out_specs=blk_q,
        out_shape=jax.ShapeDtypeStruct((b, s, HEAD_W), F32),
        scratch_shapes=[pltpu.VMEM((s, PAIR_W), BF16),
                        pltpu.VMEM((PAIR_W + SUM_ROWS, 2 * s), BF16),
                        pltpu.VMEM((2 * nblk, PAIR_W), F32),
                        pltpu.VMEM((2, MB_BLOCK, tq), F32),
                        pltpu.VMEM((nblk, tq), F32),
                        pltpu.VMEM((2, nblk, 1, tq), F32),
                        pltpu.VMEM((PAIR_W, 2 * tq), BF16),
                        pltpu.VMEM((2 * MB_BLOCK, tq), BF16),
                        pltpu.VMEM((PAIR_W + SUM_ROWS, tq), F32),
                        pltpu.VMEM((2, 1, tq), F32)]
        + [pltpu.VMEM((2, MB_BLOCK, tq), F32)] * (MB_TRIP + 1)
        + [pltpu.VMEM((2, 1, tq), F32)] * (MB_TRIP + 1),
        name="moba_prompt",
        compiler_params=_params(("arbitrary", "arbitrary", "arbitrary")),
    )(q, kt, vt, z)


_MEAN_PAGES = 64


def _top3_lanes(s, n_valid):
    width = s.shape[1]
    lane = _iota(s.shape, 1)
    s = jnp.where(lane < n_valid, s, -jnp.inf)
    picks = []
    for _ in range(MB_TOPK):
        mx = jnp.max(s, axis=1, keepdims=True)
        idx = jnp.min(jnp.where(s == mx, lane, width), axis=1, keepdims=True)
        s = jnp.where(lane == idx, -jnp.inf, s)
        picks.append(idx)
    return picks


def _mb_select_kernel(pt_ref, *refs, n_blocks):
    pages = refs[:_MEAN_PAGES]
    q_ref, idx_ref, km_sc = refs[_MEAN_PAGES:]
    s = pl.program_id(1)
    per_blk = MB_BLOCK // PAGE
    blk_per_step = _MEAN_PAGES // per_blk

    @pl.when(s == 0)
    def _():
        km_sc[...] = jnp.zeros_like(km_sc)

    lane = _iota((HEAD_W, PAIR_W), 1)
    km = km_sc[...]
    for j in range(blk_per_step):
        both = sum(pages[per_blk * j + e][...] for e in range(per_blk))
        km = jnp.where(lane == s * blk_per_step + j, jnp.sum(both, axis=1, keepdims=True) * (1.0 / MB_BLOCK), km)
    km_sc[...] = km

    @pl.when(s == pl.num_programs(1) - 1)
    def _():
        qbd = _block_diag_queries(q_ref[...], PAIR_W, MB_DIM)
        sc = jnp.dot(qbd, km, precision=lax.Precision.HIGHEST, preferred_element_type=F32)
        picks = _top3_lanes(sc, n_blocks)
        lane_o = _iota((PAIR_W, PAIR_W), 1)
        out = jnp.zeros((PAIR_W, PAIR_W), jnp.int32)
        for k, pk in enumerate(picks):
            out = jnp.where(lane_o == k, pk, out)
        idx_ref[...] = out


def _mb_select(q, cache_k, page_table):
    nb = q.shape[0]
    n_pages = page_table.shape[1]
    steps = n_pages // _MEAN_PAGES
    n_blocks = n_pages * PAGE // MB_BLOCK
    assert n_pages % _MEAN_PAGES == 0 and MB_TOPK <= n_blocks <= PAIR_W
    page_specs = [pl.BlockSpec((None, HEAD_W, PAGE), lambda bi, s, pt, j=j: (pt[bi * n_pages + s * _MEAN_PAGES + j], 0, 0))
                  for j in range(_MEAN_PAGES)]
    return pl.pallas_call(
        functools.partial(_mb_select_kernel, n_blocks=n_blocks),
        grid_spec=pltpu.PrefetchScalarGridSpec(
            num_scalar_prefetch=1,
            grid=(nb, steps),
            in_specs=page_specs + [pl.BlockSpec((None, SAMPLE_PAD, HEAD_W), lambda bi, s, pt: (bi, 0, 0))],
            out_specs=pl.BlockSpec((None, PAIR_W, PAIR_W), lambda bi, s, pt: (bi, 0, 0)),
            scratch_shapes=[pltpu.VMEM((HEAD_W, PAIR_W), F32)],
        ),
        out_shape=jax.ShapeDtypeStruct((nb, PAIR_W, PAIR_W), jnp.int32),
        name="moba_select",
        compiler_params=_params(("arbitrary", "arbitrary")),
    )(page_table.reshape(-1), *([cache_k] * _MEAN_PAGES), q)


def _mb_sample_kernel(pt_ref, sel_ref, q_ref, kn_ref, vn_ref, z_ref, idx_ref, ck_hbm, cv_hbm, o_ref,
                      kbuf, vbuf, sem, kpad, vpad, s_sc, *, n_pages, t_real):
    b = pl.program_id(0)
    per_blk = MB_BLOCK // PAGE
    n_grp = MB_TOPK * t_real
    n_sel = n_grp * MB_BLOCK
    past = n_pages * PAGE

    def copies(req, slot, h, k, t, j):
        blk = sel_ref[(req * MB_HEADS * t_real + h * t_real + t) * MB_TOPK + k]
        phys = pt_ref[req * n_pages + blk * per_blk + j]
        hs = pl.ds(h * MB_DIM, MB_DIM)
        ls = pl.ds(((k * t_real + t) * per_blk + j) * PAGE, PAGE)
        return (pltpu.make_async_copy(ck_hbm.at[phys, hs, :], kbuf.at[slot, hs, ls], sem.at[0, slot]),
                pltpu.make_async_copy(cv_hbm.at[phys, hs, :], vbuf.at[slot, hs, ls], sem.at[1, slot]))

    todo = [(h, k, t, j) for h in range(MB_HEADS) for k in range(MB_TOPK) for t in range(t_real)
            for j in range(per_blk)]

    def fetch(req, slot):
        for a in todo:
            for cp in copies(req, slot, *a):
                cp.start()

    slot = b % 2

    @pl.when(b == 0)
    def _():
        fetch(0, 0)
        kpad[...] = jnp.zeros_like(kpad)
        vpad[...] = jnp.zeros_like(vpad)

    @pl.when(b + 1 < pl.num_programs(0))
    def _():
        fetch(b + 1, 1 - slot)

    kpad[0:SAMPLE_PAD, :] = kn_ref[...]
    vpad[0:SAMPLE_PAD, :] = vn_ref[...]

    nq = PAIR_W
    qbd = _block_diag_queries(q_ref[...] * (MB_DIM ** -0.5), nq, MB_DIM).astype(BF16)
    row = _iota((nq, 1), 0)
    t_row = row % SAMPLE_PAD
    slope = jnp.exp2(-(row // SAMPLE_PAD + 1).astype(F32))

    key = _iota((nq, PAGE), 1)
    s_new = _dot_nt(qbd, kpad[...]) - slope * (t_row - key).astype(F32)
    s_new = jnp.where((key <= t_row) & (key < t_real), s_new, NEG)
    s_sc[:, n_sel:n_sel + PAGE] = s_new
    m = jnp.max(s_new, axis=1, keepdims=True)

    for a in todo:
        for cp in copies(b, slot, *a):
            cp.wait()

    key_in_blk = _iota((nq, MB_BLOCK), 1)
    for g in range(n_grp):
        k, t = divmod(g, t_real)
        lanes = slice(g * MB_BLOCK, (g + 1) * MB_BLOCK)
        pos = idx_ref[:, k:k + 1] * MB_BLOCK + key_in_blk
        sg = _dot(qbd, kbuf[slot, :, lanes]) - slope * (past + t_row - pos).astype(F32)
        sg = jnp.where(t_row == t, sg, NEG)
        s_sc[:, lanes] = sg
        m = jnp.maximum(m, jnp.max(sg, axis=1, keepdims=True))

    pr = jnp.exp(s_sc[...] - m)
    num = _dot_nt(pr[:, :n_sel], vbuf[slot]) + _dot(pr[:, n_sel:], vpad[...])
    res = num / jnp.sum(pr, axis=1, keepdims=True)
    o_ref[...] = _gather_heads(res, MB_DIM) * _silu(z_ref[...])


def _mb_sample(q, kn, vn, z, idx, cache_k, cache_v, page_table, t_real):
    nb = q.shape[0]
    n_pages = page_table.shape[1]
    n_sel = MB_TOPK * t_real * MB_BLOCK
    sel = idx[:, :MB_HEADS * SAMPLE_PAD, :MB_TOPK].reshape(nb, MB_HEADS, SAMPLE_PAD, MB_TOPK)[:, :, :t_real]
    blk = pl.BlockSpec((None, SAMPLE_PAD, HEAD_W), lambda bi, pt, sl: (bi, 0, 0))
    hbm = pl.BlockSpec(memory_space=pl.ANY)
    return pl.pallas_call(
        functools.partial(_mb_sample_kernel, n_pages=n_pages, t_real=t_real),
        grid_spec=pltpu.PrefetchScalarGridSpec(
            num_scalar_prefetch=2,
            grid=(nb,),
            in_specs=[blk, blk, blk, blk,
                      pl.BlockSpec((None, PAIR_W, PAIR_W), lambda bi, pt, sl: (bi, 0, 0)),
                      hbm, hbm],
            out_specs=blk,
            scratch_shapes=[pltpu.VMEM((2, HEAD_W, n_sel), F32),
                            pltpu.VMEM((2, HEAD_W, n_sel), F32),
                            pltpu.SemaphoreType.DMA((2, 2)),
                            pltpu.VMEM((PAGE, HEAD_W), F32),
                            pltpu.VMEM((PAGE, HEAD_W), F32),
                            pltpu.VMEM((PAIR_W, n_sel + PAGE), F32)],
        ),
        out_shape=jax.ShapeDtypeStruct((nb, SAMPLE_PAD, HEAD_W), F32),
        name="moba_sample",
        compiler_params=_params(("arbitrary",), 48),
    )(page_table.reshape(-1), sel.reshape(-1), q, kn, vn, z, idx, cache_k, cache_v)


def _pad_cols(w, width):
    return jnp.pad(w, ((0, 0), (0, width - w.shape[1])))


def _pages_feature_major(cache):
    n_pool, page, heads, dim = cache.shape
    return jnp.transpose(cache, (0, 2, 3, 1)).reshape(n_pool, heads * dim, page)


def kernel(x_prompt, x_sample, cache_sb_k, cache_sb_v, cache_mb_k, cache_mb_v, state_ml_C, state_ml_n, state_ml_m, state_gla_S, page_table, c_prompt, c_sample, w_mod, b_mod, norm_g, final_g, w_in_even, w_out_even, ml_b_i, ml_b_f, ml_gn, w_in_odd, w_out_odd, gla_w_g2, gla_b_g, gla_gn):
    bsz, seq, _ = x_prompt.shape
    db, t_real, _ = x_sample.shape
    mp = bsz * seq
    ms = db * SAMPLE_PAD

    rows = -(-(bsz + db) // 8) * 8
    c_all = jnp.pad(jnp.concatenate([c_prompt, c_sample], axis=0), ((0, rows - bsz - db), (0, 0)))
    mod = _modulation(c_all, w_mod, b_mod)
    mod_p = mod[:, :bsz, None, :]
    mod_s = jnp.repeat(mod[:, bsz:bsz + db], SAMPLE_PAD, axis=1)[:, None]

    xp = x_prompt.reshape(mp, D_MODEL)
    xs = jnp.pad(x_sample, ((0, 0), (0, SAMPLE_PAD - t_real), (0, 0))).reshape(ms, D_MODEL)
    tm = 512

    w_even = jnp.concatenate([w_in_even[0, :, :9 * HEAD_W], _pad_cols(w_in_even[0, :, 9 * HEAD_W:], PAIR_W)],
                             axis=1).astype(BF16)
    widths_even = (HEAD_W,) * 9 + (PAIR_W,)
    gate_bias = _pad_cols(jnp.concatenate([ml_b_i[0], ml_b_f[0]])[None, :], PAIR_W)
    ml_gn0 = ml_gn[0].reshape(1, HEAD_W)
    w_out0 = w_out_even[0].astype(BF16)

    w_even_tok = jnp.concatenate([w_even[:, :HEAD_W], w_even[:, 3 * HEAD_W:]], axis=1)
    w_even_kv = w_even[:, HEAD_W:3 * HEAD_W].T
    sq, sz, mq, mk, mv, mo, mz, mg, sk, sv = _inproj(xp, mod_p[0], norm_g[0], w_even_tok, (HEAD_W,) * 7 + (PAIR_W,),
                                                     seq, tm, w_even_kv, (HEAD_W, HEAD_W))
    rp = lambda u: u.reshape(bsz, seq, HEAD_W)
    a_p = _sb_prompt(rp(sq), sk, sv, rp(sz)).reshape(mp, HEAD_W)
    zc = jnp.zeros((bsz, ML_HEADS, ML_DIM, ML_DIM), F32)
    zv = jnp.zeros((bsz, SAMPLE_PAD, ML_DIM), F32)
    b_p, mlc_p, mln_p, mlm_p = _mlstm(mq, mk, mv, mo, mz, mg, gate_bias, ml_gn0, zc, zv, zv, bsz, ML_CHUNK, ML_CHUNK)
    x1p = _outproj(a_p, b_p, xp, mod_p[0], w_out0, final_g, seq, tm, False)
    sbk_p, sbv_p = sk, sv

    sq, sk, sv, sz, mq, mk, mv, mo, mz, mg = _inproj(xs, mod_s[0], norm_g[0], w_even, widths_even, ms, ms)
    r3 = lambda u: u.reshape(db, SAMPLE_PAD, HEAD_W)
    a_s = _sb_sample(r3(sq), r3(sk), r3(sv), r3(sz), _pages_feature_major(cache_sb_k[0]),
                     _pages_feature_major(cache_sb_v[0]), page_table, t_real).reshape(ms, HEAD_W)
    n0 = jnp.pad(state_ml_n[0], ((0, 0), (0, SAMPLE_PAD - ML_HEADS), (0, 0)))
    m0 = jnp.broadcast_to(jnp.pad(state_ml_m[0], ((0, 0), (0, SAMPLE_PAD - ML_HEADS)))[:, :, None],
                          (db, SAMPLE_PAD, ML_DIM))
    b_s, mlc_s, mln_s, mlm_s = _mlstm(mq, mk, mv, mo, mz, mg, gate_bias, ml_gn0, state_ml_C[0], n0, m0,
                                      db, SAMPLE_PAD, t_real)
    x1s = _outproj(a_s, b_s, xs, mod_s[0], w_out0, final_g, ms, ms, False)
    sbk_s, sbv_s = r3(sk)[:, :t_real], r3(sv)[:, :t_real]

    kw = GLA_HEADS * GLA_DK
    g0 = 2 * kw + 2 * HEAD_W
    w_odd = jnp.concatenate([w_in_odd[0, :, :g0], w_in_odd[0, :, g0 + GLA_RANK:],
                             _pad_cols(w_in_odd[0, :, g0:g0 + GLA_RANK], PAIR_W)], axis=1).astype(BF16)
    widths_odd = (kw, kw) + (HEAD_W,) * 6 + (PAIR_W,)
    wg = jnp.pad(gla_w_g2[0], ((0, PAIR_W - GLA_RANK), (0, 0))).astype(BF16)
    bg = gla_b_g[0].reshape(1, kw)
    gla_gn0 = gla_gn[0].reshape(1, HEAD_W)
    w_out1 = w_out_odd[0].astype(BF16)

    c0 = 2 * kw + 3 * HEAD_W
    w_odd_tok = jnp.concatenate([w_odd[:, :c0], w_odd[:, c0 + 2 * HEAD_W:]], axis=1)
    w_odd_kv = w_odd[:, c0:c0 + 2 * HEAD_W].T
    gq, gk, gv, gz, dq, dz, gg, dk, dv = _inproj(x1p, mod_p[1], norm_g[1], w_odd_tok,
                                                 (kw, kw) + (HEAD_W,) * 4 + (PAIR_W,), seq, tm,
                                                 w_odd_kv, (HEAD_W, HEAD_W))
    s_zero = jnp.zeros((bsz, 2, PAIR_W, GLA_DV), F32)
    c_p, glas_p = _gla(gq, gk, gv, gz, gg, wg, bg, gla_gn0, s_zero, bsz, GLA_BLOCK, GLA_SUB, GLA_BLOCK)
    d_p = _moba_prompt(rp(dq), dk, dv, rp(dz)).reshape(mp, HEAD_W)
    y_p = _outproj(c_p, d_p, x1p, mod_p[1], w_out1, final_g, seq, tm, True)
    mbk_p, mbv_p = dk, dv

    gq, gk, gv, gz, dq, dk, dv, dz, gg = _inproj(x1s, mod_s[1], norm_g[1], w_odd, widths_odd, ms, ms)
    c_s, glas_s = _gla(gq, gk, gv, gz, gg, wg, bg, gla_gn0, state_gla_S[0].reshape(db, 2, PAIR_W, GLA_DV),
                       db, SAMPLE_PAD, SAMPLE_PAD, t_real)
    ck = _pages_feature_major(cache_mb_k[0])
    cv = _pages_feature_major(cache_mb_v[0])
    idx = _mb_select(r3(dq), ck, page_table)
    d_s = _mb_sample(r3(dq), r3(dk), r3(dv), r3(dz), idx, ck, cv, page_table, t_real).reshape(ms, HEAD_W)
    y_s = _outproj(c_s, d_s, x1s, mod_s[1], w_out1, final_g, ms, ms, True)
    mbk_s, mbv_s = r3(dk)[:, :t_real], r3(dv)[:, :t_real]

    kv_p = lambda u, h, d: jnp.transpose(u.reshape(1, bsz, h, d, seq), (0, 1, 4, 2, 3))
    kv_s = lambda u, h, d: u.reshape(1, db, t_real, h, d)
    return (y_p.reshape(bsz, seq, D_MODEL), y_s.reshape(db, SAMPLE_PAD, D_MODEL)[:, :t_real],
            kv_p(sbk_p, SB_HEADS, SB_DIM), kv_p(sbv_p, SB_HEADS, SB_DIM),
            kv_s(sbk_s, SB_HEADS, SB_DIM), kv_s(sbv_s, SB_HEADS, SB_DIM),
            mlc_p[None], mln_p[None, :, :ML_HEADS], mlm_p[None, :, :ML_HEADS, 0],
            mlc_s[None], mln_s[None, :, :ML_HEADS], mlm_s[None, :, :ML_HEADS, 0],
            glas_p.reshape(1, bsz, GLA_HEADS, GLA_DK, GLA_DV), glas_s.reshape(1, db, GLA_HEADS, GLA_DK, GLA_DV),
            kv_p(mbk_p, MB_HEADS, MB_DIM), kv_p(mbv_p, MB_HEADS, MB_DIM),
            kv_s(mbk_s, MB_HEADS, MB_DIM), kv_s(mbv_s, MB_HEADS, MB_DIM))
```
